```python
import math
import jax, jax.numpy as jnp
from jax import lax
import numpy as np

D_MODEL = 1024
BATCH = 4
SEQ = 4096
DEPTH = 4

H_DN = 4
DK_DN = 128
DV_DN = 128
DN_CONV = 4
DN_CHUNK = 64
H_FOX = 4
D_FOX = 128
D_DIFF = 64
H_DIFF = D_MODEL // (2 * D_DIFF)
ROPE_THETA = 10000.0
Q_BLOCK = 128
D_FF = 128 * ((8 * D_MODEL // 3 + 127) // 128)
FFN_CONV = 3
EPS = 1e-6

N_EVEN = (DEPTH + 1) // 2
N_ODD = DEPTH // 2
DN_QKV = H_DN * (2 * DK_DN + DV_DN)
EV_SIZES = (DN_QKV, H_DN * DV_DN, H_DN, H_DN, H_FOX * D_FOX, H_FOX * D_FOX, H_FOX * D_FOX, H_FOX)
EV_IN = DN_QKV + H_DN * DV_DN + 2 * H_DN + 3 * H_FOX * D_FOX + H_FOX
OD_IN = 3 * D_MODEL

kernel_name = 'hybrid_deltanet_fox_diffattn_convffn'


def _rmsnorm(x, w):
    xf = x.astype(jnp.float32)
    y = xf * lax.rsqrt(jnp.mean(xf * xf, axis=-1, keepdims=True) + EPS)
    return (y * w.astype(jnp.float32)).astype(x.dtype)


def _l2norm(x):
    xf = x.astype(jnp.float32)
    return (xf * lax.rsqrt(jnp.sum(xf * xf, axis=-1, keepdims=True) + EPS)).astype(x.dtype)


def _split_cols(x, sizes):
    out, start = [], 0
    for n in sizes:
        out.append(x[..., start:start + n])
        start += n
    return out


def _causal_dwconv(x, w):
    k, c = w.shape
    return lax.conv_general_dilated(x, w[:, None, :], window_strides=(1,), padding=[(k - 1, 0)],
                                    dimension_numbers=('NWC', 'WIO', 'NWC'), feature_group_count=c)


def _rope(x, pos):
    d = x.shape[-1]
    half = d // 2
    inv = 1.0 / (ROPE_THETA ** (jnp.arange(half, dtype=jnp.float32) * 2.0 / d))
    ang = pos[:, None] * inv[None, :]
    cos, sin = jnp.cos(ang), jnp.sin(ang)
    xf = x.astype(jnp.float32)
    x1, x2 = xf[..., :half], xf[..., half:]
    return jnp.concatenate([x1 * cos - x2 * sin, x1 * sin + x2 * cos], axis=-1).astype(x.dtype)


def _gated_delta_rule(q, k, v, g, beta):
    out_dtype = v.dtype
    f32 = jnp.float32
    b, h, s, dk = q.shape
    dv = v.shape[-1]
    c = DN_CHUNK
    n = s // c
    q = q.astype(f32) * dk ** -0.5
    k, v, g, beta = k.astype(f32), v.astype(f32), g.astype(f32), beta.astype(f32)
    chunk = lambda t: t.reshape(b, h, n, c, *t.shape[3:])
    q, k, v, g, beta = chunk(q), chunk(k), chunk(v), chunk(g), chunk(beta)
    kb = k * beta[..., None]
    vb = v * beta[..., None]
    g = jnp.cumsum(g, axis=-1)
    incl = jnp.tril(jnp.ones((c, c), dtype=bool))
    strict = jnp.tril(jnp.ones((c, c), dtype=bool), -1)
    gdiff = g[..., :, None] - g[..., None, :]
    decay = jnp.where(incl, jnp.exp(jnp.where(incl, gdiff, 0.0)), 0.0)
    lower = jnp.where(strict, jnp.einsum('bhncd,bhnsd->bhncs', kb, k) * decay, 0.0)
    t_mat = lower + jnp.eye(c, dtype=f32)
    u = lax.linalg.triangular_solve(t_mat, vb, left_side=True, lower=True, unit_diagonal=True)
    w = lax.linalg.triangular_solve(t_mat, kb * jnp.exp(g)[..., None], left_side=True, lower=True,
                                    unit_diagonal=True)
    qk = jnp.einsum('bhncd,bhnsd->bhncs', q, k) * decay
    qg = q * jnp.exp(g)[..., None]
    kd = k * jnp.exp(g[..., -1:] - g)[..., None]
    g_last = jnp.exp(g[..., -1])

    def step(state, xs):
        u_c, w_c, qk_c, qg_c, kd_c, gl_c = xs
        v_new = u_c - jnp.einsum('bhck,bhkv->bhcv', w_c, state)
        o_c = jnp.einsum('bhck,bhkv->bhcv', qg_c, state) + jnp.einsum('bhcs,bhsv->bhcv', qk_c, v_new)
        state = state * gl_c[..., None, None] + jnp.einsum('bhck,bhcv->bhkv', kd_c, v_new)
        return state, o_c

    lead = lambda t: jnp.moveaxis(t, 2, 0)
    s0 = jnp.zeros((b, h, dk, dv), f32)
    _, o = lax.scan(step, s0, (lead(u), lead(w), lead(qk), lead(qg), lead(kd), lead(g_last)))
    o = jnp.moveaxis(o, 0, 2).reshape(b, h, s, dv)
    return o.astype(out_dtype)


def _fox_attention(q, k, v, c):
    b, h, s, d = q.shape
    nb = s // Q_BLOCK
    scale = d ** -0.5
    kpos = jnp.arange(s)

    def blk(i):
        start = i * Q_BLOCK
        qb = lax.dynamic_slice_in_dim(q, start, Q_BLOCK, axis=2)
        cb = lax.dynamic_slice_in_dim(c, start, Q_BLOCK, axis=2)
        logits = (jnp.einsum('bhqd,bhkd->bhqk', qb, k).astype(jnp.float32) * scale
                  + cb[..., :, None] - c[..., None, :])
        qpos = start + jnp.arange(Q_BLOCK)
        logits = jnp.where(kpos[None, :] <= qpos[:, None], logits, -jnp.inf)
        p = jax.nn.softmax(logits, axis=-1)
        return jnp.einsum('bhqk,bhkd->bhqd', p.astype(v.dtype), v)

    o = lax.map(blk, jnp.arange(nb))
    return jnp.moveaxis(o, 0, 2).reshape(b, h, s, d)


def _diff_attention(q, k, v, lam):
    b, h, _, s, dh = q.shape
    nb = s // Q_BLOCK
    scale = dh ** -0.5
    kpos = jnp.arange(s)

    def blk(i):
        start = i * Q_BLOCK
        qb = lax.dynamic_slice_in_dim(q, start, Q_BLOCK, axis=3)
        logits = jnp.einsum('bhmqd,bhmkd->bhmqk', qb, k).astype(jnp.float32) * scale
        qpos = start + jnp.arange(Q_BLOCK)
        logits = jnp.where(kpos[None, :] <= qpos[:, None], logits, -jnp.inf)
        p = jax.nn.softmax(logits, axis=-1)
        pd = p[:, :, 0] - lam * p[:, :, 1]
        return jnp.einsum('bhqk,bhkd->bhqd', pd.astype(v.dtype), v)

    o = lax.map(blk, jnp.arange(nb))
    return jnp.moveaxis(o, 0, 2).reshape(b, h, s, v.shape[-1])


def _deltanet_fox_mixer(hn, w_in, conv_w, a_log, dt_bias, dn_norm_w, f_bias, w_out):
    b, s, _ = hn.shape
    f32 = jnp.float32
    proj = hn @ w_in
    qkv_dn, z_dn, beta_lg, a_lg, q_f, k_f, v_f, f_lg = _split_cols(proj, EV_SIZES)
    heads = lambda t, nh: t.reshape(b, s, nh, -1).transpose(0, 2, 1, 3)
    qkv_dn = jax.nn.silu(_causal_dwconv(qkv_dn, conv_w))
    q_d, k_d, v_d = _split_cols(qkv_dn, (H_DN * DK_DN, H_DN * DK_DN, H_DN * DV_DN))
    q_d = _l2norm(heads(q_d, H_DN))
    k_d = _l2norm(heads(k_d, H_DN))
    v_d = heads(v_d, H_DN)
    beta = jax.nn.sigmoid(beta_lg.astype(f32)).transpose(0, 2, 1)
    g = (-jnp.exp(a_log.astype(f32))
         * jax.nn.softplus(a_lg.astype(f32) + dt_bias.astype(f32))).transpose(0, 2, 1)
    o_d = _gated_delta_rule(q_d, k_d, v_d, g, beta).transpose(0, 2, 1, 3)
    o_d = _rmsnorm(o_d, dn_norm_w) * jax.nn.silu(z_dn.reshape(b, s, H_DN, DV_DN))
    log_f = jax.nn.log_sigmoid(f_lg.astype(f32) + f_bias.astype(f32))
    c = jnp.cumsum(log_f, axis=1).transpose(0, 2, 1)
    o_f = _fox_attention(heads(q_f, H_FOX), heads(k_f, H_FOX), heads(v_f, H_FOX), c)
    o_f = o_f.transpose(0, 2, 1, 3).reshape(b, s, H_FOX * D_FOX)
    o = jnp.concatenate([o_d.reshape(b, s, H_DN * DV_DN).astype(o_f.dtype), o_f], axis=-1)
    return o @ w_out


def _diff_mixer(hn, w_in, lam_vecs, subln_w, w_out, lambda_init):
    b, s, _ = hn.shape
    q, k, v = _split_cols(hn @ w_in, (D_MODEL, D_MODEL, D_MODEL))
    q = q.reshape(b, s, H_DIFF, 2, D_DIFF).transpose(0, 2, 3, 1, 4)
    k = k.reshape(b, s, H_DIFF, 2, D_DIFF).transpose(0, 2, 3, 1, 4)
    v = v.reshape(b, s, H_DIFF, 2 * D_DIFF).transpose(0, 2, 1, 3)
    pos = jnp.arange(s, dtype=jnp.float32)
    q = _rope(q, pos)
    k = _rope(k, pos)
    lv = lam_vecs.astype(jnp.float32)
    lam = jnp.exp(jnp.sum(lv[0] * lv[1])) - jnp.exp(jnp.sum(lv[2] * lv[3])) + lambda_init
    o = _diff_attention(q, k, v, lam)
    o = _rmsnorm(o, subln_w) * (1.0 - lambda_init)
    o = o.transpose(0, 2, 1, 3).reshape(b, s, D_MODEL)
    return o @ w_out


def _conv_ffn(hn, w_up, conv_w, conv_b, w_down):
    gate, val = _split_cols(hn @ w_up, (D_FF, D_FF))
    gate = _causal_dwconv(gate, conv_w) + conv_b
    return (jax.nn.silu(gate) * val) @ w_down


def setup_inputs(seed: int = 0) -> dict:
    key = jax.random.key(seed)
    ks = jax.random.split(key, 20)
    f32 = jnp.float32
    nrm = lambda kk, shape, scale: jax.random.normal(kk, shape, f32) * scale
    dt = jnp.exp(jax.random.uniform(ks[5], (N_EVEN, H_DN), f32, math.log(1e-3), math.log(1e-1)))
    return {
        'x': nrm(ks[0], (BATCH, SEQ, D_MODEL), 1.0),
        'ev_norm_w': 1.0 + nrm(ks[1], (N_EVEN, D_MODEL), 0.02),
        'ev_w_in': nrm(ks[2], (N_EVEN, D_MODEL, EV_IN), D_MODEL ** -0.5),
        'dn_conv_w': nrm(ks[3], (N_EVEN, DN_CONV, DN_QKV), DN_CONV ** -0.5),
        'dn_a_log': jnp.log(jax.random.uniform(ks[4], (N_EVEN, H_DN), f32, 1.0, 16.0)),
        'dn_dt_bias': dt + jnp.log(-jnp.expm1(-dt)),
        'dn_norm_w': 1.0 + nrm(ks[6], (N_EVEN, DV_DN), 0.02),
        'fox_f_bias': 2.0 + nrm(ks[7], (N_EVEN, H_FOX), 0.5),
        'ev_w_out': nrm(ks[8], (N_EVEN, D_MODEL, D_MODEL), D_MODEL ** -0.5),
        'od_norm_w': 1.0 + nrm(ks[9], (N_ODD, D_MODEL), 0.02),
        'od_w_in': nrm(ks[10], (N_ODD, D_MODEL, OD_IN), D_MODEL ** -0.5),
        'diff_lambda': nrm(ks[11], (N_ODD, 4, D_DIFF), 0.1),
        'diff_subln_w': 1.0 + nrm(ks[12], (N_ODD, 2 * D_DIFF), 0.02),
        'od_w_out': nrm(ks[13], (N_ODD, D_MODEL, D_MODEL), D_MODEL ** -0.5),
        'ffn_norm_w': 1.0 + nrm(ks[14], (DEPTH, D_MODEL), 0.02),
        'ffn_w_up': nrm(ks[15], (DEPTH, D_MODEL, 2 * D_FF), D_MODEL ** -0.5),
        'ffn_conv_w': nrm(ks[16], (DEPTH, FFN_CONV, D_FF), FFN_CONV ** -0.5),
        'ffn_conv_b': nrm(ks[17], (DEPTH, D_FF), 0.02),
        'ffn_w_down': nrm(ks[18], (DEPTH, D_FF, D_MODEL), D_FF ** -0.5),
        'final_norm_w': 1.0 + nrm(ks[19], (D_MODEL,), 0.02),
    }


def reference(x, ev_norm_w, ev_w_in, dn_conv_w, dn_a_log, dn_dt_bias, dn_norm_w, fox_f_bias, ev_w_out,
              od_norm_w, od_w_in, diff_lambda, diff_subln_w, od_w_out,
              ffn_norm_w, ffn_w_up, ffn_conv_w, ffn_conv_b, ffn_w_down, final_norm_w):
    for i in range(DEPTH):
        j = i // 2
        if i % 2 == 0:
            x = x + _deltanet_fox_mixer(_rmsnorm(x, ev_norm_w[j]), ev_w_in[j], dn_conv_w[j], dn_a_log[j],
                                        dn_dt_bias[j], dn_norm_w[j], fox_f_bias[j], ev_w_out[j])
        else:
            lambda_init = 0.8 - 0.6 * math.exp(-0.3 * i)
            x = x + _diff_mixer(_rmsnorm(x, od_norm_w[j]), od_w_in[j], diff_lambda[j], diff_subln_w[j],
                                od_w_out[j], lambda_init)
        x = x + _conv_ffn(_rmsnorm(x, ffn_norm_w[i]), ffn_w_up[i], ffn_conv_w[i], ffn_conv_b[i], ffn_w_down[i])
    return _rmsnorm(x, final_norm_w)
```

```python
import functools
import math

import jax
import jax.numpy as jnp
from jax import lax
from jax.experimental import pallas as pl
from jax.experimental.pallas import tpu as pltpu

F32 = jnp.float32
BF16 = jnp.bfloat16

D_MODEL = 1024
H_DN = 4
DK_DN = 128
DV_DN = 128
DN_CONV = 4
DN_CHUNK = 64
H_FOX = 4
D_FOX = 128
D_DIFF = 64
H_DIFF = D_MODEL // (2 * D_DIFF)
ROPE_THETA = 10000.0
D_FF = 128 * ((8 * D_MODEL // 3 + 127) // 128)
FFN_CONV = 3
EPS = 1e-6
LOG2E = 1.4426950408889634

LANES = 128
BF16_SUBLANES = 16
VMEM_LIMIT = 48 * 1024 * 1024

DN_QKV = H_DN * (2 * DK_DN + DV_DN)
EV_BIG = DN_QKV + H_DN * DV_DN + 3 * H_FOX * D_FOX
_QD, _KD, _VD, _ZD = 0, 4, 8, 12
_QF, _KF, _VF = 16, 20, 24
_G_BETA, _G_DECAY, _G_FOX = 0, 4, 8


def _cparams(sem):
    return pltpu.CompilerParams(dimension_semantics=sem, vmem_limit_bytes=VMEM_LIMIT)


def _rms(x, w):
    return x * lax.rsqrt(jnp.mean(x * x, axis=-1, keepdims=True) + EPS) * w


def _sigmoid(x):
    return 1.0 / (1.0 + jnp.exp(-x))


def _silu(x):
    return x * _sigmoid(x)


def _dot(a, b):
    return jnp.dot(a, b, preferred_element_type=F32)


def _dot_nt(a, b):
    return lax.dot_general(a, b, (((1,), (1,)), ((), ())), preferred_element_type=F32)


def _ev_proj_kernel(x_ref, nw_ref, w_ref, cs_ref, ws_ref, big_ref, small_ref, hn_ref):
    @pl.when(pl.program_id(1) == 0)
    def _():
        hn = _rms(x_ref[...], nw_ref[...]).astype(BF16)
        hn_ref[...] = hn
        small_ref[...] = _dot(hn, ws_ref[...])

    big_ref[...] = (_dot(hn_ref[...], w_ref[...]) * cs_ref[...]).astype(BF16)


def _ev_proj(x, nw, w_big, colscale, w_small, *, tm=512, tn=512):
    m = x.shape[0]
    n = w_big.shape[1]
    return pl.pallas_call(
        _ev_proj_kernel,
        grid=(m // tm, n // tn),
        in_specs=[
            pl.BlockSpec((tm, D_MODEL), lambda i, j: (i, 0)),
            pl.BlockSpec((1, D_MODEL), lambda i, j: (0, 0)),
            pl.BlockSpec((D_MODEL, tn), lambda i, j: (0, j)),
            pl.BlockSpec((1, tn), lambda i, j: (0, j)),
            pl.BlockSpec((D_MODEL, LANES), lambda i, j: (0, 0)),
        ],
        out_specs=[
            pl.BlockSpec((tm, tn), lambda i, j: (i, j)),
            pl.BlockSpec((tm, LANES), lambda i, j: (i, 0)),
        ],
        out_shape=[
            jax.ShapeDtypeStruct((m, n), BF16),
            jax.ShapeDtypeStruct((m, LANES), F32),
        ],
        scratch_shapes=[pltpu.VMEM((tm, D_MODEL), BF16)],
        compiler_params=_cparams(("parallel", "arbitrary")),
        name="ev_proj",
    )(x, nw, w_big, colscale, w_small)


def _gate_kernel(lg_ref, nega_ref, dtb_ref, fb_ref, col_ref, row_ref, *, seq):
    t = lg_ref[...]
    lane = lax.broadcasted_iota(jnp.int32, t.shape, 1)
    row = lax.broadcasted_iota(jnp.int32, t.shape, 0)
    is_decay = (lane >= _G_DECAY) & (lane < _G_FOX)
    is_fox = (lane >= _G_FOX) & (lane < _G_FOX + H_FOX)
    beta = _sigmoid(t)
    td = t + dtb_ref[...]
    softplus = jnp.maximum(td, 0.0) + jnp.log(1.0 + jnp.exp(-jnp.abs(td)))
    g = nega_ref[...] * softplus
    tf = t + fb_ref[...]
    logf = (jnp.minimum(tf, 0.0) - jnp.log(1.0 + jnp.exp(-jnp.abs(tf)))) * LOG2E
    v = jnp.where(is_decay, g, jnp.where(is_fox, logf, 0.0))
    pos = jnp.where(is_decay, row % DN_CHUNK, jnp.where(is_fox, row, 0))
    k = 1
    while k < seq:
        shifted = pltpu.roll(v, k, 0)
        v = v + jnp.where(pos >= k, shifted, 0.0)
        k *= 2
    out = jnp.where(lane < _G_DECAY, beta, v)
    col_ref[...] = out
    row_ref[...] = out.T[:16, :]


def _gates(small, nega, dtb, fb, *, batch, seq):
    return pl.pallas_call(
        functools.partial(_gate_kernel, seq=seq),
        grid=(batch,),
        in_specs=[
            pl.BlockSpec((seq, LANES), lambda b: (b, 0)),
            pl.BlockSpec((1, LANES), lambda b: (0, 0)),
            pl.BlockSpec((1, LANES), lambda b: (0, 0)),
            pl.BlockSpec((1, LANES), lambda b: (0, 0)),
        ],
        out_specs=[
            pl.BlockSpec((seq, LANES), lambda b: (b, 0)),
            pl.BlockSpec((None, 16, seq), lambda b: (b, 0, 0)),
        ],
        out_shape=[
            jax.ShapeDtypeStruct((batch * seq, LANES), F32),
            jax.ShapeDtypeStruct((batch, 16, seq), F32),
        ],
        compiler_params=_cparams(("parallel",)),
        name="gates",
    )(small, nega, dtb, fb)


def _dn_kernel(q_ref, k_ref, v_ref, qh_ref, kh_ref, vh_ref, z_ref, cq_ref, ck_ref, cv_ref,
               gcol_ref, grow_ref, nw_ref, o_ref, state_ref, *, ts):
    h = pl.program_id(1)
    si = pl.program_id(2)
    c = DN_CHUNK
    nc = ts // c
    halo = BF16_SUBLANES

    @pl.when(si == 0)
    def _():
        state_ref[...] = jnp.zeros_like(state_ref)

    def conv_silu(x_ref, halo_ref, cw_ref):
        x = x_ref[...].astype(F32)
        prev = jnp.where(si == 0, 0.0, halo_ref[...].astype(F32))
        xe = jnp.concatenate([prev, x], axis=0)
        cw = cw_ref[...]
        y = cw[DN_CONV - 1:DN_CONV] * x
        for j in range(DN_CONV - 1):
            off = halo - (DN_CONV - 1) + j
            y = y + cw[j:j + 1] * xe[off:off + ts]
        return _silu(y)

    def l2n(x):
        return x * lax.rsqrt(jnp.sum(x * x, axis=-1, keepdims=True) + EPS)

    q = l2n(conv_silu(q_ref, qh_ref, cq_ref)) * (DK_DN ** -0.5)
    k = l2n(conv_silu(k_ref, kh_ref, ck_ref))
    v = conv_silu(v_ref, vh_ref, cv_ref)

    gates = gcol_ref[...]
    lane = lax.broadcasted_iota(jnp.int32, gates.shape, 1)
    beta = jnp.sum(jnp.where(lane == _G_BETA + h, gates, 0.0), axis=1, keepdims=True)
    gc = jnp.sum(jnp.where(lane == _G_DECAY + h, gates, 0.0), axis=1, keepdims=True)
    g_row = grow_ref[:, pl.ds(_G_DECAY + h, 1), :]

    gc3 = gc.reshape(nc, c, 1)
    gl3 = gc3[:, c - 1:c, :]
    eg = jnp.exp(gc)
    kb = k * beta
    vb = v * beta
    kbg = kb * eg
    qg = q * eg
    kd = (k.reshape(nc, c, DK_DN) * jnp.exp(gl3 - gc3))
    egl = jnp.exp(gl3)

    r3 = lax.broadcasted_iota(jnp.int32, (nc, c, c), 1)
    c3 = lax.broadcasted_iota(jnp.int32, (nc, c, c), 2)
    incl = r3 >= c3
    decay = jnp.where(incl, jnp.exp(jnp.where(incl, gc3 - g_row, 0.0)), 0.0)

    k3 = k.reshape(nc, c, DK_DN).astype(BF16)
    kb3 = kb.reshape(nc, c, DK_DN).astype(BF16)
    q3 = q.reshape(nc, c, DK_DN).astype(BF16)
    bdot_nt = lambda a, b: jnp.einsum('cid,cjd->cij', a, b, preferred_element_type=F32)
    bdot = lambda a, b: jnp.einsum('cij,cjd->cid', a, b, preferred_element_type=F32)
    lower = jnp.where(r3 > c3, bdot_nt(kb3, k3) * decay, 0.0)
    qk = (bdot_nt(q3, k3) * decay).astype(BF16)

    def level_mask(s):
        return ((r3 // (2 * s)) == (c3 // (2 * s))) & ((r3 // s) % 2 == 1) & ((c3 // s) % 2 == 0)

    xs = -jnp.where(level_mask(1), lower, 0.0)
    s = 2
    while s < c:
        cs = jnp.where(level_mask(s), lower, 0.0)
        y = cs + bdot(cs.astype(BF16), xs.astype(BF16))
        xs = xs - y - bdot(xs.astype(BF16), y.astype(BF16))
        s *= 2

    rhs = jnp.concatenate([vb.reshape(nc, c, DV_DN), kbg.reshape(nc, c, DK_DN)], axis=2)
    uw = rhs + bdot(xs.astype(BF16), rhs.astype(BF16))
    u3 = uw[:, :, :DV_DN]
    w3 = uw[:, :, DV_DN:].astype(BF16)
    qg3 = qg.reshape(nc, c, DK_DN).astype(BF16)
    kd3 = kd.astype(BF16)

    state = state_ref[...]
    outs = []
    for n in range(nc):
        wq = jnp.concatenate([w3[n], qg3[n]], axis=0)
        ws = _dot(wq, state.astype(BF16))
        v_new = u3[n] - ws[:c]
        vnb = v_new.astype(BF16)
        outs.append(ws[c:] + _dot(qk[n], vnb))
        state = state * egl[n] + lax.dot_general(kd3[n], vnb, (((0,), (0,)), ((), ())),
                                                 preferred_element_type=F32)
    state_ref[...] = state
    o = jnp.concatenate(outs, axis=0)
    o = _rms(o, nw_ref[...]) * _silu(z_ref[...].astype(F32))
    o_ref[...] = o.astype(BF16)


def _deltanet(big, conv_w, gcol, grow4, norm_w, *, batch, seq, ts=512):
    m = batch * seq
    ns = seq // ts
    nc = ts // DN_CHUNK
    hb = ts // BF16_SUBLANES

    def blk(col0):
        return pl.BlockSpec((ts, LANES), lambda b, h, s: (b * ns + s, col0 + h))

    def halo(col0):
        return pl.BlockSpec((BF16_SUBLANES, LANES),
                            lambda b, h, s: (jnp.maximum((b * ns + s) * hb - 1, 0), col0 + h))

    def cw(col0):
        return pl.BlockSpec((DN_CONV, LANES), lambda b, h, s: (0, col0 + h))

    return pl.pallas_call(
        functools.partial(_dn_kernel, ts=ts),
        grid=(batch, H_DN, ns),
        in_specs=[
            blk(_QD), blk(_KD), blk(_VD), halo(_QD), halo(_KD), halo(_VD), blk(_ZD),
            cw(_QD), cw(_KD), cw(_VD),
            pl.BlockSpec((ts, LANES), lambda b, h, s: (b * ns + s, 0)),
            pl.BlockSpec((None, nc, 16, DN_CHUNK), lambda b, h, s: (b, s, 0, 0)),
            pl.BlockSpec((1, DV_DN), lambda b, h, s: (0, 0)),
        ],
        out_specs=pl.BlockSpec((ts, LANES), lambda b, h, s: (b * ns + s, h)),
        out_shape=jax.ShapeDtypeStruct((m, H_DN * DV_DN), BF16),
        scratch_shapes=[pltpu.VMEM((DK_DN, DV_DN), F32)],
        compiler_params=_cparams(("parallel", "parallel", "arbitrary")),
        name="deltanet",
    )(big, big, big, big, big, big, big, conv_w, conv_w, conv_w, gcol, grow4, norm_w)


def _softmax_step(t, v, m_ref, l_ref, acc_ref):
    m_prev = m_ref[...]
    m_new = jnp.maximum(m_prev, jnp.max(t, axis=1, keepdims=True))
    alpha = jnp.exp2(m_prev - m_new)
    p = jnp.exp2(t - m_new[:, :1])
    l_ref[...] = alpha * l_ref[...] + jnp.sum(p, axis=1, keepdims=True)
    acc_ref[...] = alpha * acc_ref[...] + _dot(p.astype(BF16), v)
    m_ref[...] = m_new


def _causal_mask(t):
    r = lax.broadcasted_iota(jnp.int32, t.shape, 0)
    c = lax.broadcasted_iota(jnp.int32, t.shape, 1)
    return jnp.where(c <= r, t, -jnp.inf)


def _reset(m_ref, l_ref, acc_ref):
    m_ref[...] = jnp.full_like(m_ref, -jnp.inf)
    l_ref[...] = jnp.zeros_like(l_ref)
    acc_ref[...] = jnp.zeros_like(acc_ref)


def _fox_kernel(q_ref, k_ref, v_ref, c_ref, o_ref, m_ref, l_ref, acc_ref, *, tq):
    qi = pl.program_id(1)
    for h in range(H_FOX):
        cols = slice(h * D_FOX, (h + 1) * D_FOX)
        q = q_ref[:, cols]
        _reset(m_ref, l_ref, acc_ref)

        def step(kb, masked):
            ks = pl.multiple_of(kb * tq, tq)
            t = _dot_nt(q, k_ref[pl.ds(ks, tq), cols]) - c_ref[_G_FOX + h:_G_FOX + h + 1, pl.ds(ks, tq)]
            if masked:
                t = _causal_mask(t)
            _softmax_step(t, v_ref[pl.ds(ks, tq), cols], m_ref, l_ref, acc_ref)

        def body(kb, carry):
            step(kb, False)
            return carry

        lax.fori_loop(0, qi, body, 0)
        step(qi, True)
        o_ref[:, cols] = (acc_ref[...] / l_ref[...]).astype(BF16)


def _fox(big, grow, *, batch, seq, tq=512):
    nq = seq // tq
    big3 = big.reshape(batch, seq, EV_BIG)
    w = H_FOX * D_FOX
    return pl.pallas_call(
        functools.partial(_fox_kernel, tq=tq),
        grid=(batch, nq),
        in_specs=[
            pl.BlockSpec((None, tq, w), lambda b, i: (b, i, _QF // H_FOX)),
            pl.BlockSpec((None, seq, w), lambda b, i: (b, 0, _KF // H_FOX)),
            pl.BlockSpec((None, seq, w), lambda b, i: (b, 0, _VF // H_FOX)),
            pl.BlockSpec((None, 16, seq), lambda b, i: (b, 0, 0)),
        ],
        out_specs=pl.BlockSpec((None, tq, w), lambda b, i: (b, i, 0)),
        out_shape=jax.ShapeDtypeStruct((batch, seq, w), BF16),
        scratch_shapes=[pltpu.VMEM((tq, LANES), F32)] * 3,
        compiler_params=_cparams(("parallel", "arbitrary")),
        name="fox_attn",
    )(big3, big3, big3, grow).reshape(batch * seq, w)


def _diff_kernel(q_ref, k_ref, v_ref, lv_ref, sw_ref, o_ref, m1, l1, a1, m2, l2, a2,
                 *, tq, hp, lambda_init):
    qi = pl.program_id(2)
    lv = lv_ref[...]
    lam = (jnp.exp(jnp.sum(lv[0:1] * lv[1:2], axis=1, keepdims=True))
           - jnp.exp(jnp.sum(lv[2:3] * lv[3:4], axis=1, keepdims=True)) + lambda_init)
    for h in range(hp):
        cols = slice(h * LANES, (h + 1) * LANES)
        q = q_ref[:, cols]
        lane = lax.broadcasted_iota(jnp.int32, q.shape, 1)
        qa = jnp.where(lane < D_DIFF, q, jnp.zeros_like(q))
        qb = jnp.where(lane >= D_DIFF, q, jnp.zeros_like(q))
        _reset(m1, l1, a1)
        _reset(m2, l2, a2)

        def step(kb, masked):
            ks = pl.multiple_of(kb * tq, tq)
            k = k_ref[pl.ds(ks, tq), cols]
            v = v_ref[pl.ds(ks, tq), cols]
            ta = _dot_nt(qa, k)
            tb = _dot_nt(qb, k)
            if masked:
                ta = _causal_mask(ta)
                tb = _causal_mask(tb)
            _softmax_step(ta, v, m1, l1, a1)
            _softmax_step(tb, v, m2, l2, a2)

        def body(kb, carry):
            step(kb, False)
            return carry

        lax.fori_loop(0, qi, body, 0)
        step(qi, True)
        o = a1[...] / l1[...] - lam * (a2[...] / l2[...])
        o = _rms(o, sw_ref[...]) * (1.0 - lambda_init)
        o_ref[:, cols] = o.astype(BF16)


def _diff_attn(qkv, lam_vecs, subln_w, lambda_init, *, batch, seq, tq=512, hp=4):
    nq = seq // tq
    qkv3 = qkv.reshape(batch, seq, 3 * D_MODEL)
    w = hp * LANES
    ng = D_MODEL // w
    return pl.pallas_call(
        functools.partial(_diff_kernel, tq=tq, hp=hp, lambda_init=lambda_init),
        grid=(batch, ng, nq),
        in_specs=[
            pl.BlockSpec((None, tq, w), lambda b, g, i: (b, i, g)),
            pl.BlockSpec((None, seq, w), lambda b, g, i: (b, 0, ng + g)),
            pl.BlockSpec((None, seq, w), lambda b, g, i: (b, 0, 2 * ng + g)),
            pl.BlockSpec((4, D_DIFF), lambda b, g, i: (0, 0)),
            pl.BlockSpec((1, 2 * D_DIFF), lambda b, g, i: (0, 0)),
        ],
        out_specs=pl.BlockSpec((None, tq, w), lambda b, g, i: (b, i, g)),
        out_shape=jax.ShapeDtypeStruct((batch, seq, D_MODEL), BF16),
        scratch_shapes=[pltpu.VMEM((tq, LANES), F32)] * 6,
        compiler_params=_cparams(("parallel", "parallel", "arbitrary")),
        name="diff_attn",
    )(qkv3, qkv3, qkv3, lam_vecs, subln_w).reshape(batch * seq, D_MODEL)


def _od_proj_kernel(x_ref, nw_ref, w_ref, cos_ref, sin_ref, o_ref, hn_ref, *, tn):
    @pl.when(pl.program_id(1) == 0)
    def _():
        hn_ref[...] = _rms(x_ref[...], nw_ref[...]).astype(BF16)

    y = _dot(hn_ref[...], w_ref[...])
    cos = cos_ref[...]
    sin = sin_ref[...]
    half = D_DIFF // 2
    lane = lax.broadcasted_iota(jnp.int32, cos.shape, 1)
    first_half = (lane % D_DIFF) < half
    for g in range(tn // LANES):
        cols = slice(g * LANES, (g + 1) * LANES)
        yg = y[:, cols]
        rot = jnp.where(first_half, pltpu.roll(yg, LANES - half, 1), pltpu.roll(yg, half, 1))
        o_ref[:, cols] = (yg * cos + rot * sin).astype(BF16)


def _od_proj(x, nw, w, cos_t, sin_t, *, seq, tm=512, tn=512):
    m = x.shape[0]
    n = w.shape[1]
    sb = seq // tm
    per = D_MODEL // tn
    return pl.pallas_call(
        functools.partial(_od_proj_kernel, tn=tn),
        grid=(m // tm, n // tn),
        in_specs=[
            pl.BlockSpec((tm, D_MODEL), lambda i, j: (i, 0)),
            pl.BlockSpec((1, D_MODEL), lambda i, j: (0, 0)),
            pl.BlockSpec((D_MODEL, tn), lambda i, j: (0, j)),
            pl.BlockSpec((None, tm, LANES), lambda i, j: (j // per, i % sb, 0)),
            pl.BlockSpec((None, tm, LANES), lambda i, j: (j // per, i % sb, 0)),
        ],
        out_specs=pl.BlockSpec((tm, tn), lambda i, j: (i, j)),
        out_shape=jax.ShapeDtypeStruct((m, n), BF16),
        scratch_shapes=[pltpu.VMEM((tm, D_MODEL), BF16)],
        compiler_params=_cparams(("parallel", "arbitrary")),
        name="od_proj",
    )(x, nw, w, cos_t, sin_t)


def _rope_tables(seq):
    half = D_DIFF // 2
    inv = 1.0 / (ROPE_THETA ** (jnp.arange(half, dtype=F32) * 2.0 / D_DIFF))
    ang = jnp.arange(seq, dtype=F32)[:, None] * inv[None, :]
    cos = jnp.tile(jnp.cos(ang), (1, LANES // half))
    sin = jnp.tile(jnp.concatenate([-jnp.sin(ang), jnp.sin(ang)], axis=1), (1, LANES // D_DIFF))
    qs = (D_DIFF ** -0.5) * LOG2E
    cos_t = jnp.stack([cos * qs, cos, jnp.ones_like(cos)])
    sin_t = jnp.stack([sin * qs, sin, jnp.zeros_like(sin)])
    return cos_t, sin_t


def _out_proj_kernel(*refs, n_in):
    x_ref = refs[0]
    a_refs = refs[1:1 + n_in]
    w_refs = refs[1 + n_in:1 + 2 * n_in]
    o_ref = refs[1 + 2 * n_in]
    acc = x_ref[...]
    for a_ref, w_ref in zip(a_refs, w_refs):
        acc = acc + _dot(a_ref[...], w_ref[...])
    o_ref[...] = acc


def _out_proj(x, acts, w, *, tm=512):
    m = x.shape[0]
    n_in = len(acts)
    kw = acts[0].shape[1]
    in_specs = [pl.BlockSpec((tm, D_MODEL), lambda i: (i, 0))]
    in_specs += [pl.BlockSpec((tm, kw), lambda i: (i, 0)) for _ in acts]
    in_specs += [pl.BlockSpec((kw, D_MODEL), functools.partial(lambda i, r: (r, 0), r=r))
                 for r in range(n_in)]
    return pl.pallas_call(
        functools.partial(_out_proj_kernel, n_in=n_in),
        grid=(m // tm,),
        in_specs=in_specs,
        out_specs=pl.BlockSpec((tm, D_MODEL), lambda i: (i, 0)),
        out_shape=jax.ShapeDtypeStruct((m, D_MODEL), F32),
        compiler_params=_cparams(("parallel",)),
        name="out_proj",
    )(x, *acts, *([w] * n_in))


def _ffn_kernel(x_ref, xp_ref, nw_ref, wg_ref, wv_ref, cw_ref, cb_ref, wd_ref, fw_ref, o_ref,
                hn_ref, acc_ref, *, tm, rows_per_seq, final_norm):
    i = pl.program_id(0)
    j = pl.program_id(1)
    halo = BF16_SUBLANES

    @pl.when(j == 0)
    def _():
        hn_ref[:halo, :] = _rms(xp_ref[...], nw_ref[...]).astype(BF16)
        hn_ref[halo:, :] = _rms(x_ref[...], nw_ref[...]).astype(BF16)
        acc_ref[...] = jnp.zeros_like(acc_ref)

    hn = hn_ref[...]
    ge = _dot(hn, wg_ref[...])
    seq_start = (i % rows_per_seq) == 0
    r = lax.broadcasted_iota(jnp.int32, ge.shape, 0)
    ge = jnp.where(seq_start & (r < halo), 0.0, ge)
    cw = cw_ref[...]
    gate = cb_ref[...] + cw[FFN_CONV - 1:FFN_CONV] * ge[halo:]
    for t in range(FFN_CONV - 1):
        off = halo - (FFN_CONV - 1) + t
        gate = gate + cw[t:t + 1] * ge[off:off + tm]
    val = _dot(hn[halo:], wv_ref[...])
    hmid = (_silu(gate) * val).astype(BF16)
    acc_ref[...] += _dot(hmid, wd_ref[...])

    @pl.when(j == pl.num_programs(1) - 1)
    def _():
        y = x_ref[...] + acc_ref[...]
        if final_norm:
            y = _rms(y, fw_ref[...])
        o_ref[...] = y


def _ffn(x, nw, w_up, conv_w, conv_b, w_down, final_w, *, seq, final_norm, tm=512, tn=256):
    m = x.shape[0]
    nj = D_FF // tn
    hb = tm // BF16_SUBLANES
    return pl.pallas_call(
        functools.partial(_ffn_kernel, tm=tm, rows_per_seq=seq // tm, final_norm=final_norm),
        grid=(m // tm, nj),
        in_specs=[
            pl.BlockSpec((tm, D_MODEL), lambda i, j: (i, 0)),
            pl.BlockSpec((BF16_SUBLANES, D_MODEL), lambda i, j: (jnp.maximum(i * hb - 1, 0), 0)),
            pl.BlockSpec((1, D_MODEL), lambda i, j: (0, 0)),
            pl.BlockSpec((D_MODEL, tn), lambda i, j: (0, j)),
            pl.BlockSpec((D_MODEL, tn), lambda i, j: (0, nj + j)),
            pl.BlockSpec((FFN_CONV, tn), lambda i, j: (0, j)),
            pl.BlockSpec((1, tn), lambda i, j: (0, j)),
            pl.BlockSpec((tn, D_MODEL), lambda i, j: (j, 0)),
            pl.BlockSpec((1, D_MODEL), lambda i, j: (0, 0)),
        ],
        out_specs=pl.BlockSpec((tm, D_MODEL), lambda i, j: (i, 0)),
        out_shape=jax.ShapeDtypeStruct((m, D_MODEL), F32),
        scratch_shapes=[pltpu.VMEM((tm + BF16_SUBLANES, D_MODEL), BF16),
                        pltpu.VMEM((tm, D_MODEL), F32)],
        compiler_params=_cparams(("parallel", "arbitrary")),
        name="conv_ffn",
    )(x, x, nw, w_up, w_up, conv_w, conv_b, w_down, final_w)


def _even_layer(x, norm_w, w_in, conv_w, a_log, dt_bias, dn_norm_w, f_bias, w_out, *, batch, seq):
    gate0 = DN_QKV + H_DN * DV_DN
    fox0 = gate0 + 2 * H_DN
    fox1 = fox0 + 3 * H_FOX * D_FOX
    w_big = jnp.concatenate([w_in[:, :gate0], w_in[:, fox0:fox1]], axis=1).astype(BF16)
    w_small = jnp.concatenate(
        [w_in[:, gate0:fox0], w_in[:, fox1:], jnp.zeros((D_MODEL, LANES - 2 * H_DN - H_FOX), F32)],
        axis=1).astype(BF16)
    colscale = jnp.ones((1, EV_BIG), F32).at[:, _QF * LANES:_KF * LANES].set((D_FOX ** -0.5) * LOG2E)
    big, small = _ev_proj(x, norm_w[None, :], w_big, colscale, w_small)

    def lanes(vals, start):
        return jnp.zeros((1, LANES), F32).at[0, start:start + vals.shape[0]].set(vals)

    gcol, grow = _gates(small, lanes(-jnp.exp(a_log), _G_DECAY), lanes(dt_bias, _G_DECAY),
                        lanes(f_bias, _G_FOX), batch=batch, seq=seq)
    grow4 = grow.reshape(batch, 16, seq // DN_CHUNK, DN_CHUNK).transpose(0, 2, 1, 3)
    o_d = _deltanet(big, conv_w, gcol, grow4, dn_norm_w[None, :], batch=batch, seq=seq)
    o_f = _fox(big, grow, batch=batch, seq=seq)
    return _out_proj(x, [o_d, o_f], w_out.astype(BF16))


def _odd_layer(x, norm_w, w_in, lam_vecs, subln_w, w_out, lambda_init, rope, *, batch, seq):
    qkv = _od_proj(x, norm_w[None, :], w_in.astype(BF16), rope[0], rope[1], seq=seq)
    o = _diff_attn(qkv, lam_vecs, subln_w[None, :], lambda_init, batch=batch, seq=seq)
    return _out_proj(x, [o], w_out.astype(BF16))


def kernel(x, ev_norm_w, ev_w_in, dn_conv_w, dn_a_log, dn_dt_bias, dn_norm_w, fox_f_bias, ev_w_out,
           od_norm_w, od_w_in, diff_lambda, diff_subln_w, od_w_out,
           ffn_norm_w, ffn_w_up, ffn_conv_w, ffn_conv_b, ffn_w_down, final_norm_w):
    batch, seq, _ = x.shape
    depth = ffn_norm_w.shape[0]
    h = x.reshape(batch * seq, D_MODEL)
    rope = _rope_tables(seq)
    for i in range(depth):
        j = i // 2
        if i % 2 == 0:
            h = _even_layer(h, ev_norm_w[j], ev_w_in[j], dn_conv_w[j], dn_a_log[j], dn_dt_bias[j],
                            dn_norm_w[j], fox_f_bias[j], ev_w_out[j], batch=batch, seq=seq)
        else:
            lambda_init = 0.8 - 0.6 * math.exp(-0.3 * i)
            h = _odd_layer(h, od_norm_w[j], od_w_in[j], diff_lambda[j], diff_subln_w[j], od_w_out[j],
                           lambda_init, rope, batch=batch, seq=seq)
        h = _ffn(h, ffn_norm_w[i][None, :], ffn_w_up[i].astype(BF16), ffn_conv_w[i],
                 ffn_conv_b[i][None, :], ffn_w_down[i].astype(BF16), final_norm_w[None, :],
                 seq=seq, final_norm=(i == depth - 1))
    return h.reshape(batch, seq, D_MODEL)
```

```python
import functools
import math

import jax
import jax.numpy as jnp
from jax import lax
from jax.experimental import pallas as pl
from jax.experimental.pallas import tpu as pltpu

F32 = jnp.float32
BF16 = jnp.bfloat16

D_MODEL = 1024
H_DN = 4
DK_DN = 128
DV_DN = 128
DN_CONV = 4
DN_CHUNK = 64
H_FOX = 4
D_FOX = 128
D_DIFF = 64
H_DIFF = D_MODEL // (2 * D_DIFF)
ROPE_THETA = 10000.0
D_FF = 128 * ((8 * D_MODEL // 3 + 127) // 128)
FFN_CONV = 3
EPS = 1e-6
LOG2E = 1.4426950408889634

LANES = 128
MXU_N = 256
BF16_SUBLANES = 16
VMEM_LIMIT = 48 * 1024 * 1024

DN_QKV = H_DN * (2 * DK_DN + DV_DN)
EV_GATE0 = DN_QKV + H_DN * DV_DN
EV_FOX0 = EV_GATE0 + 2 * H_DN
EV_BIG = EV_GATE0 + 2 * H_FOX * D_FOX
_QD, _KD, _VD, _ZD = 0, 4, 8, 12
_QF, _KF = 16, 20
_G_BETA, _G_DECAY, _G_FOX = 0, 4, 8


def _cparams(sem):
    return pltpu.CompilerParams(dimension_semantics=sem, vmem_limit_bytes=VMEM_LIMIT)


def _rms(x, w):
    return x * lax.rsqrt(jnp.mean(x * x, axis=-1, keepdims=True) + EPS) * w


def _sigmoid(x):
    return 1.0 / (1.0 + jnp.exp(-x))


def _silu(x):
    return x * _sigmoid(x)


def _dot(a, b):
    return jnp.dot(a, b, preferred_element_type=F32)


def _dot_nt(a, b):
    return lax.dot_general(a, b, (((1,), (1,)), ((), ())), preferred_element_type=F32)


def _ev_proj_kernel(x_ref, nw_ref, wa_ref, wb_ref, wvt_ref, ws_ref, big_ref, vt_ref, small_ref,
                    hn_ref, *, na, nb):
    j = pl.program_id(1)

    @pl.when(j == 0)
    def _():
        hn = _rms(x_ref[...], nw_ref[...]).astype(BF16)
        hn_ref[...] = hn
        small_ref[...] = _dot(hn, ws_ref[...])

    @pl.when(j < na)
    def _():
        big_ref[...] = _dot(hn_ref[...], wa_ref[...]).astype(BF16)

    @pl.when((j >= na) & (j < na + nb))
    def _():
        scale = jnp.where(j == na, (D_FOX ** -0.5) * LOG2E, 1.0)
        big_ref[...] = (_dot(hn_ref[...], wb_ref[...]) * scale).astype(BF16)

    @pl.when(j == na + nb)
    def _():
        vt_ref[...] = _dot_nt(wvt_ref[...], hn_ref[...]).astype(BF16)


def _ev_proj(x, nw, w_in, w_b, w_vt, w_small, layer, *, tm=512, tn=512):
    m = x.shape[0]
    na = EV_GATE0 // tn
    nb = w_b.shape[2] // tn
    wv = w_vt.shape[1]
    return pl.pallas_call(
        functools.partial(_ev_proj_kernel, na=na, nb=nb),
        grid=(m // tm, na + nb + 1),
        in_specs=[
            pl.BlockSpec((tm, D_MODEL), lambda i, j: (i, 0)),
            pl.BlockSpec((None, 1, D_MODEL), lambda i, j: (layer, 0, 0)),
            pl.BlockSpec((None, D_MODEL, tn), lambda i, j: (layer, 0, jnp.minimum(j, na - 1))),
            pl.BlockSpec((None, D_MODEL, tn), lambda i, j: (layer, 0, jnp.clip(j - na, 0, nb - 1))),
            pl.BlockSpec((None, wv, D_MODEL), lambda i, j: (layer, 0, 0)),
            pl.BlockSpec((None, D_MODEL, LANES), lambda i, j: (layer, 0, 0)),
        ],
        out_specs=[
            pl.BlockSpec((tm, tn), lambda i, j: (i, jnp.minimum(j, na + nb - 1))),
            pl.BlockSpec((wv, tm), lambda i, j: (0, i)),
            pl.BlockSpec((tm, LANES), lambda i, j: (i, 0)),
        ],
        out_shape=[
            jax.ShapeDtypeStruct((m, EV_BIG), BF16),
            jax.ShapeDtypeStruct((wv, m), BF16),
            jax.ShapeDtypeStruct((m, LANES), F32),
        ],
        scratch_shapes=[pltpu.VMEM((tm, D_MODEL), BF16)],
        compiler_params=_cparams(("parallel", "arbitrary")),
        name="ev_proj",
    )(x, nw, w_in, w_b, w_vt, w_small)


def _gate_kernel(lg_ref, nega_ref, dtb_ref, fb_ref, col_ref, row_ref, cb_ref, *, seq):
    t = lg_ref[...]
    lane = lax.broadcasted_iota(jnp.int32, t.shape, 1)
    row = lax.broadcasted_iota(jnp.int32, t.shape, 0)
    is_decay = (lane >= _G_DECAY) & (lane < _G_FOX)
    is_fox = (lane >= _G_FOX) & (lane < _G_FOX + H_FOX)
    beta = _sigmoid(t)
    td = t + dtb_ref[...]
    softplus = jnp.maximum(td, 0.0) + jnp.log(1.0 + jnp.exp(-jnp.abs(td)))
    g = nega_ref[...] * softplus
    tf = t + fb_ref[...]
    logf = (jnp.minimum(tf, 0.0) - jnp.log(1.0 + jnp.exp(-jnp.abs(tf)))) * LOG2E
    v = jnp.where(is_decay, g, jnp.where(is_fox, logf, 0.0))
    pos = jnp.where(is_decay, row % DN_CHUNK, jnp.where(is_fox, row, 0))
    k = 1
    while k < seq:
        shifted = pltpu.roll(v, k, 0)
        v = v + jnp.where(pos >= k, shifted, 0.0)
        k *= 2
    out = jnp.where(lane < _G_DECAY, beta, v)
    col_ref[...] = out
    row_ref[...] = out.T[:16, :]
    for h in range(H_FOX):
        cb_ref[h] = jnp.broadcast_to(v[:, _G_FOX + h:_G_FOX + h + 1], t.shape)


def _gates(small, nega, dtb, fb, *, batch, seq):
    return pl.pallas_call(
        functools.partial(_gate_kernel, seq=seq),
        grid=(batch,),
        in_specs=[
            pl.BlockSpec((seq, LANES), lambda b: (b, 0)),
            pl.BlockSpec((1, LANES), lambda b: (0, 0)),
            pl.BlockSpec((1, LANES), lambda b: (0, 0)),
            pl.BlockSpec((1, LANES), lambda b: (0, 0)),
        ],
        out_specs=[
            pl.BlockSpec((seq, LANES), lambda b: (b, 0)),
            pl.BlockSpec((None, 16, seq), lambda b: (b, 0, 0)),
            pl.BlockSpec((None, H_FOX, seq, LANES), lambda b: (b, 0, 0, 0)),
        ],
        out_shape=[
            jax.ShapeDtypeStruct((batch * seq, LANES), F32),
            jax.ShapeDtypeStruct((batch, 16, seq), F32),
            jax.ShapeDtypeStruct((batch, H_FOX, seq, LANES), F32),
        ],
        compiler_params=_cparams(("parallel",)),
        name="gates",
    )(small, nega, dtb, fb)


def _dn_kernel(q_ref, k_ref, v_ref, qh_ref, kh_ref, vh_ref, z_ref, cq_ref, ck_ref, cv_ref,
               gcol_ref, grow_ref, nw_ref, o_ref, state_ref, *, ts):
    h = pl.program_id(1)
    si = pl.program_id(2)
    c = DN_CHUNK
    nc = ts // c
    halo = BF16_SUBLANES

    @pl.when(si == 0)
    def _():
        state_ref[...] = jnp.zeros_like(state_ref)

    def conv_silu(x_ref, halo_ref, cw_ref):
        x = x_ref[...].astype(F32)
        prev = jnp.where(si == 0, 0.0, halo_ref[...].astype(F32))
        xe = jnp.concatenate([prev, x], axis=0)
        cw = cw_ref[...]
        y = cw[DN_CONV - 1:DN_CONV] * x
        for j in range(DN_CONV - 1):
            off = halo - (DN_CONV - 1) + j
            y = y + cw[j:j + 1] * xe[off:off + ts]
        return _silu(y)

    def l2n(x):
        return x * lax.rsqrt(jnp.sum(x * x, axis=-1, keepdims=True) + EPS)

    q = l2n(conv_silu(q_ref, qh_ref, cq_ref)) * (DK_DN ** -0.5)
    k = l2n(conv_silu(k_ref, kh_ref, ck_ref))
    v = conv_silu(v_ref, vh_ref, cv_ref)

    gates = gcol_ref[...]
    lane = lax.broadcasted_iota(jnp.int32, gates.shape, 1)
    beta = jnp.sum(jnp.where(lane == _G_BETA + h, gates, 0.0), axis=1, keepdims=True)
    gc = jnp.sum(jnp.where(lane == _G_DECAY + h, gates, 0.0), axis=1, keepdims=True)
    g_row = grow_ref[:, pl.ds(_G_DECAY + h, 1), :]

    gc3 = gc.reshape(nc, c, 1)
    gl3 = gc3[:, c - 1:c, :]
    eg = jnp.exp(gc)
    kb = k * beta
    vb = v * beta
    kbg = kb * eg
    qg = q * eg
    kd = (k.reshape(nc, c, DK_DN) * jnp.exp(gl3 - gc3))
    egl = jnp.exp(gl3)

    r3 = lax.broadcasted_iota(jnp.int32, (nc, c, c), 1)
    c3 = lax.broadcasted_iota(jnp.int32, (nc, c, c), 2)
    incl = r3 >= c3
    decay = jnp.where(incl, jnp.exp(jnp.where(incl, gc3 - g_row, 0.0)), 0.0)

    k3 = k.reshape(nc, c, DK_DN).astype(BF16)
    kb3 = kb.reshape(nc, c, DK_DN).astype(BF16)
    q3 = q.reshape(nc, c, DK_DN).astype(BF16)
    bdot_nt = lambda a, b: jnp.einsum('cid,cjd->cij', a, b, preferred_element_type=F32)
    bdot = lambda a, b: jnp.einsum('cij,cjd->cid', a, b, preferred_element_type=F32)
    lower = jnp.where(r3 > c3, bdot_nt(kb3, k3) * decay, 0.0)
    qk = (bdot_nt(q3, k3) * decay).astype(BF16)

    def level_mask(s):
        return ((r3 // (2 * s)) == (c3 // (2 * s))) & ((r3 // s) % 2 == 1) & ((c3 // s) % 2 == 0)

    xs = -jnp.where(level_mask(1), lower, 0.0)
    s = 2
    while s < c:
        cs = jnp.where(level_mask(s), lower, 0.0)
        y = cs + bdot(cs.astype(BF16), xs.astype(BF16))
        xs = xs - y - bdot(xs.astype(BF16), y.astype(BF16))
        s *= 2

    rhs = jnp.concatenate([vb.reshape(nc, c, DV_DN), kbg.reshape(nc, c, DK_DN)], axis=2)
    uw = rhs + bdot(xs.astype(BF16), rhs.astype(BF16))
    u3 = uw[:, :, :DV_DN]
    w3 = uw[:, :, DV_DN:].astype(BF16)
    qg3 = qg.reshape(nc, c, DK_DN).astype(BF16)
    kd3 = kd.astype(BF16)

    state = state_ref[...]
    outs = []
    for n in range(nc):
        wq = jnp.concatenate([w3[n], qg3[n]], axis=0)
        ws = _dot(wq, state.astype(BF16))
        v_new = u3[n] - ws[:c]
        vnb = v_new.astype(BF16)
        outs.append(ws[c:] + _dot(qk[n], vnb))
        state = state * egl[n] + lax.dot_general(kd3[n], vnb, (((0,), (0,)), ((), ())),
                                                 preferred_element_type=F32)
    state_ref[...] = state
    o = jnp.concatenate(outs, axis=0)
    o = _rms(o, nw_ref[...]) * _silu(z_ref[...].astype(F32))
    o_ref[...] = o.astype(BF16)


def _deltanet(big, conv_w, gcol, grow4, norm_w, layer, *, batch, seq, ts=512):
    m = batch * seq
    ns = seq // ts
    nc = ts // DN_CHUNK
    hb = ts // BF16_SUBLANES

    def blk(col0):
        return pl.BlockSpec((ts, LANES), lambda b, h, s: (b * ns + s, col0 + h))

    def halo(col0):
        return pl.BlockSpec((BF16_SUBLANES, LANES),
                            lambda b, h, s: (jnp.maximum((b * ns + s) * hb - 1, 0), col0 + h))

    def cw(col0):
        return pl.BlockSpec((None, DN_CONV, LANES), lambda b, h, s: (layer, 0, col0 + h))

    return pl.pallas_call(
        functools.partial(_dn_kernel, ts=ts),
        grid=(batch, H_DN, ns),
        in_specs=[
            blk(_QD), blk(_KD), blk(_VD), halo(_QD), halo(_KD), halo(_VD), blk(_ZD),
            cw(_QD), cw(_KD), cw(_VD),
            pl.BlockSpec((ts, LANES), lambda b, h, s: (b * ns + s, 0)),
            pl.BlockSpec((None, nc, 16, DN_CHUNK), lambda b, h, s: (b, s, 0, 0)),
            pl.BlockSpec((None, 1, DV_DN), lambda b, h, s: (layer, 0, 0)),
        ],
        out_specs=pl.BlockSpec((ts, LANES), lambda b, h, s: (b * ns + s, h)),
        out_shape=jax.ShapeDtypeStruct((m, H_DN * DV_DN), BF16),
        scratch_shapes=[pltpu.VMEM((DK_DN, DV_DN), F32)],
        compiler_params=_cparams(("parallel", "parallel", "arbitrary")),
        name="deltanet",
    )(big, big, big, big, big, big, big, conv_w, conv_w, conv_w, gcol, grow4, norm_w)


def _attn_kernel(*refs, tq, ck, diff, lambda_init):
    if diff:
        q_ref, k_ref, vt_ref, lv_ref, sw_ref, o_ref, qz_ref, m_ref, l_ref, acc_ref = refs
        rsub = MXU_N // 2
    else:
        q_ref, k_ref, vt_ref, cb_ref, o_ref, qz_ref, m_ref, l_ref, acc_ref = refs
        rsub = MXU_N
    nsub = tq // rsub
    ahead = 3
    kpi = 2
    qi = pl.program_id(2)

    for r in range(nsub):
        rows = q_ref[r * rsub:(r + 1) * rsub, :]
        if diff:
            lane = lax.broadcasted_iota(jnp.int32, rows.shape, 1)
            qz_ref[r, :rsub, :] = jnp.where(lane < D_DIFF, rows, jnp.zeros_like(rows))
            qz_ref[r, rsub:, :] = jnp.where(lane >= D_DIFF, rows, jnp.zeros_like(rows))
        else:
            qz_ref[r] = rows
    m_ref[...] = jnp.full_like(m_ref, -jnp.inf)
    l_ref[...] = jnp.zeros_like(l_ref)
    acc_ref[...] = jnp.zeros_like(acc_ref)

    def run(tiles):
        loaded = {}

        def operands(t):
            if t not in loaded:
                ks = tiles[t][0]
                cb = None
                if not diff:
                    cb = cb_ref[pl.ds(ks, ck), :]
                    cb = jnp.concatenate([cb] * (MXU_N // LANES), axis=1)
                loaded[t] = (k_ref[pl.ds(ks, ck), :], vt_ref[:, pl.ds(ks, ck)], cb)
            return loaded[t]

        items = [(t, r) for t, (_, plan) in enumerate(tiles) for r in range(nsub)
                 if plan[r] != "skip"]
        scores = {}

        def issue(item):
            scores[item] = _dot_nt(operands(item[0])[0], qz_ref[item[1]])

        for item in items[:ahead]:
            issue(item)
        for idx, (t, r) in enumerate(items):
            if idx + ahead < len(items):
                issue(items[idx + ahead])
            s = scores.pop((t, r))
            _, vt, cb = operands(t)
            plan = tiles[t][1]
            if not diff:
                s = s - cb
            if plan[r] is not None:
                kpos = lax.broadcasted_iota(jnp.int32, s.shape, 0) + plan[r]
                qpos = lax.broadcasted_iota(jnp.int32, s.shape, 1)
                if diff:
                    qpos = qpos % rsub
                s = jnp.where(kpos <= qpos + r * rsub, s, -jnp.inf)
            m_prev = m_ref[r]
            m_new = jnp.maximum(m_prev, jnp.max(s, axis=0, keepdims=True))
            alpha = jnp.exp2(m_prev - m_new)
            p = jnp.exp2(s - m_new)
            l_ref[r] = alpha * l_ref[r] + jnp.sum(p, axis=0, keepdims=True)
            acc_ref[r] = alpha * acc_ref[r] + _dot(vt, p.astype(BF16))
            m_ref[r] = m_new

    def body(t, carry):
        base = t * (kpi * ck)
        run([(pl.multiple_of(base + u * ck, ck), [None] * nsub) for u in range(kpi)])
        return carry

    lax.fori_loop(0, qi * (tq // (kpi * ck)), body, 0)
    diag = []
    for jd in range(tq // ck):
        plan = []
        for r in range(nsub):
            if jd * ck > r * rsub + rsub - 1:
                plan.append("skip")
            elif jd * ck + ck - 1 <= r * rsub:
                plan.append(None)
            else:
                plan.append(jd * ck)
        diag.append((pl.multiple_of(qi * tq + jd * ck, ck), plan))
    run(diag)

    if diff:
        lv = lv_ref[...]
        lam = (jnp.exp(jnp.sum(lv[0:1] * lv[1:2], axis=1, keepdims=True))
               - jnp.exp(jnp.sum(lv[2:3] * lv[3:4], axis=1, keepdims=True)) + lambda_init)
    for r in range(nsub):
        ot = acc_ref[r] / l_ref[r]
        if diff:
            o = (ot[:, :rsub] - lam * ot[:, rsub:]).T
            o = _rms(o, sw_ref[...]) * (1.0 - lambda_init)
        else:
            o = ot.T
        o_ref[r * rsub:(r + 1) * rsub, :] = o.astype(BF16)


def _attn_scratch(tq, rsub):
    nsub = tq // rsub
    return [pltpu.VMEM((nsub, MXU_N, LANES), BF16), pltpu.VMEM((nsub, 1, MXU_N), F32),
            pltpu.VMEM((nsub, 1, MXU_N), F32), pltpu.VMEM((nsub, LANES, MXU_N), F32)]


def _fox(big, vt, cb, *, batch, seq, tq=1024, ck=256):
    nq = seq // tq
    big3 = big.reshape(batch, seq, EV_BIG)
    return pl.pallas_call(
        functools.partial(_attn_kernel, tq=tq, ck=ck, diff=False, lambda_init=0.0),
        grid=(batch, H_FOX, nq),
        in_specs=[
            pl.BlockSpec((None, tq, LANES), lambda b, h, i: (b, i, _QF + h)),
            pl.BlockSpec((None, seq, LANES), lambda b, h, i: (b, 0, _KF + h)),
            pl.BlockSpec((LANES, seq), lambda b, h, i: (h, b)),
            pl.BlockSpec((None, None, seq, LANES), lambda b, h, i: (b, h, 0, 0)),
        ],
        out_specs=pl.BlockSpec((None, tq, LANES), lambda b, h, i: (b, i, h)),
        out_shape=jax.ShapeDtypeStruct((batch, seq, H_FOX * D_FOX), BF16),
        scratch_shapes=_attn_scratch(tq, MXU_N),
        compiler_params=_cparams(("parallel", "parallel", "arbitrary")),
        name="fox_attn",
    )(big3, big3, vt, cb).reshape(batch * seq, H_FOX * D_FOX)


def _diff_attn(qk, vt, lam_vecs, subln_w, layer, lambda_init, *, batch, seq, tq=1024, ck=256):
    nq = seq // tq
    qk3 = qk.reshape(batch, seq, 2 * D_MODEL)
    return pl.pallas_call(
        functools.partial(_attn_kernel, tq=tq, ck=ck, diff=True, lambda_init=lambda_init),
        grid=(batch, H_DIFF, nq),
        in_specs=[
            pl.BlockSpec((None, tq, LANES), lambda b, h, i: (b, i, h)),
            pl.BlockSpec((None, seq, LANES), lambda b, h, i: (b, 0, H_DIFF + h)),
            pl.BlockSpec((LANES, seq), lambda b, h, i: (h, b)),
            pl.BlockSpec((None, 4, D_DIFF), lambda b, h, i: (layer, 0, 0)),
            pl.BlockSpec((None, 1, 2 * D_DIFF), lambda b, h, i: (layer, 0, 0)),
        ],
        out_specs=pl.BlockSpec((None, tq, LANES), lambda b, h, i: (b, i, h)),
        out_shape=jax.ShapeDtypeStruct((batch, seq, D_MODEL), BF16),
        scratch_shapes=_attn_scratch(tq, MXU_N // 2),
        compiler_params=_cparams(("parallel", "parallel", "arbitrary")),
        name="diff_attn",
    )(qk3, qk3, vt, lam_vecs, subln_w).reshape(batch * seq, D_MODEL)


def _od_proj_kernel(x_ref, nw_ref, w_ref, wvt_ref, cos_ref, sin_ref, o_ref, vt_ref, hn_ref,
                    *, tn, nqk):
    j = pl.program_id(1)

    @pl.when(j == 0)
    def _():
        hn_ref[...] = _rms(x_ref[...], nw_ref[...]).astype(BF16)

    @pl.when(j < nqk)
    def _():
        y = _dot(hn_ref[...], w_ref[...])
        cos = cos_ref[...]
        sin = sin_ref[...]
        half = D_DIFF // 2
        lane = lax.broadcasted_iota(jnp.int32, cos.shape, 1)
        first_half = (lane % D_DIFF) < half
        for g in range(tn // LANES):
            cols = slice(g * LANES, (g + 1) * LANES)
            yg = y[:, cols]
            rot = jnp.where(first_half, pltpu.roll(yg, LANES - half, 1), pltpu.roll(yg, half, 1))
            o_ref[:, cols] = (yg * cos + rot * sin).astype(BF16)

    @pl.when(j >= nqk)
    def _():
        vt_ref[...] = _dot_nt(wvt_ref[...], hn_ref[...]).astype(BF16)


def _od_proj(x, nw, w, w_vt, cos_t, sin_t, layer, *, seq, tm=512, tn=512):
    m = x.shape[0]
    nqk = 2 * D_MODEL // tn
    nv = D_MODEL // tn
    sb = seq // tm
    per = D_MODEL // tn
    return pl.pallas_call(
        functools.partial(_od_proj_kernel, tn=tn, nqk=nqk),
        grid=(m // tm, nqk + nv),
        in_specs=[
            pl.BlockSpec((tm, D_MODEL), lambda i, j: (i, 0)),
            pl.BlockSpec((None, 1, D_MODEL), lambda i, j: (layer, 0, 0)),
            pl.BlockSpec((None, D_MODEL, tn), lambda i, j: (layer, 0, jnp.minimum(j, nqk - 1))),
            pl.BlockSpec((None, tn, D_MODEL), lambda i, j: (layer, jnp.maximum(j - nqk, 0), 0)),
            pl.BlockSpec((None, tm, LANES), lambda i, j: (jnp.minimum(j, nqk - 1) // per, i % sb, 0)),
            pl.BlockSpec((None, tm, LANES), lambda i, j: (jnp.minimum(j, nqk - 1) // per, i % sb, 0)),
        ],
        out_specs=[
            pl.BlockSpec((tm, tn), lambda i, j: (i, jnp.minimum(j, nqk - 1))),
            pl.BlockSpec((tn, tm), lambda i, j: (jnp.maximum(j - nqk, 0), i)),
        ],
        out_shape=[
            jax.ShapeDtypeStruct((m, 2 * D_MODEL), BF16),
            jax.ShapeDtypeStruct((D_MODEL, m), BF16),
        ],
        scratch_shapes=[pltpu.VMEM((tm, D_MODEL), BF16)],
        compiler_params=_cparams(("parallel", "arbitrary")),
        name="od_proj",
    )(x, nw, w, w_vt, cos_t, sin_t)


def _rope_tables(seq):
    half = D_DIFF // 2
    inv = 1.0 / (ROPE_THETA ** (jnp.arange(half, dtype=F32) * 2.0 / D_DIFF))
    ang = jnp.arange(seq, dtype=F32)[:, None] * inv[None, :]
    cos = jnp.tile(jnp.cos(ang), (1, LANES // half))
    sin = jnp.tile(jnp.concatenate([-jnp.sin(ang), jnp.sin(ang)], axis=1), (1, LANES // D_DIFF))
    qs = (D_DIFF ** -0.5) * LOG2E
    return jnp.stack([cos * qs, cos]), jnp.stack([sin * qs, sin])


def _out_proj_kernel(*refs, n_in):
    x_ref = refs[0]
    a_refs = refs[1:1 + n_in]
    w_refs = refs[1 + n_in:1 + 2 * n_in]
    o_ref = refs[1 + 2 * n_in]
    acc = x_ref[...]
    for a_ref, w_ref in zip(a_refs, w_refs):
        acc = acc + _dot(a_ref[...], w_ref[...])
    o_ref[...] = acc


def _out_proj(x, acts, w, layer, *, tm=512):
    m = x.shape[0]
    n_in = len(acts)
    kw = acts[0].shape[1]
    in_specs = [pl.BlockSpec((tm, D_MODEL), lambda i: (i, 0))]
    in_specs += [pl.BlockSpec((tm, kw), lambda i: (i, 0)) for _ in acts]
    in_specs += [pl.BlockSpec((None, kw, D_MODEL), functools.partial(lambda i, r: (layer, r, 0), r=r))
                 for r in range(n_in)]
    return pl.pallas_call(
        functools.partial(_out_proj_kernel, n_in=n_in),
        grid=(m // tm,),
        in_specs=in_specs,
        out_specs=pl.BlockSpec((tm, D_MODEL), lambda i: (i, 0)),
        out_shape=jax.ShapeDtypeStruct((m, D_MODEL), F32),
        compiler_params=_cparams(("parallel",)),
        name="out_proj",
    )(x, *acts, *([w] * n_in))


def _ffn_kernel(x_ref, xp_ref, nw_ref, wg_ref, wv_ref, cw_ref, cb_ref, wd_ref, fw_ref, o_ref,
                hn_ref, acc_ref, *, tm, rows_per_seq, final_norm):
    i = pl.program_id(0)
    j = pl.program_id(1)
    halo = BF16_SUBLANES

    @pl.when(j == 0)
    def _():
        hn_ref[:halo, :] = _rms(xp_ref[...], nw_ref[...]).astype(BF16)
        hn_ref[halo:, :] = _rms(x_ref[...], nw_ref[...]).astype(BF16)
        acc_ref[...] = jnp.zeros_like(acc_ref)

    hn = hn_ref[...]
    ge = _dot(hn, wg_ref[...])
    seq_start = (i % rows_per_seq) == 0
    r = lax.broadcasted_iota(jnp.int32, ge.shape, 0)
    ge = jnp.where(seq_start & (r < halo), 0.0, ge)
    cw = cw_ref[...]
    gate = cb_ref[...] + cw[FFN_CONV - 1:FFN_CONV] * ge[halo:]
    for t in range(FFN_CONV - 1):
        off = halo - (FFN_CONV - 1) + t
        gate = gate + cw[t:t + 1] * ge[off:off + tm]
    val = _dot(hn[halo:], wv_ref[...])
    hmid = (_silu(gate) * val).astype(BF16)
    acc_ref[...] += _dot(hmid, wd_ref[...])

    @pl.when(j == pl.num_programs(1) - 1)
    def _():
        y = x_ref[...] + acc_ref[...]
        if final_norm:
            y = _rms(y, fw_ref[...])
        o_ref[...] = y


def _ffn(x, nw, w_up, conv_w, conv_b, w_down, final_w, layer, *, seq, final_norm, tm=512, tn=256):
    m = x.shape[0]
    nj = D_FF // tn
    hb = tm // BF16_SUBLANES
    return pl.pallas_call(
        functools.partial(_ffn_kernel, tm=tm, rows_per_seq=seq // tm, final_norm=final_norm),
        grid=(m // tm, nj),
        in_specs=[
            pl.BlockSpec((tm, D_MODEL), lambda i, j: (i, 0)),
            pl.BlockSpec((BF16_SUBLANES, D_MODEL), lambda i, j: (jnp.maximum(i * hb - 1, 0), 0)),
            pl.BlockSpec((None, 1, D_MODEL), lambda i, j: (layer, 0, 0)),
            pl.BlockSpec((None, D_MODEL, tn), lambda i, j: (layer, 0, j)),
            pl.BlockSpec((None, D_MODEL, tn), lambda i, j: (layer, 0, nj + j)),
            pl.BlockSpec((None, FFN_CONV, tn), lambda i, j: (layer, 0, j)),
            pl.BlockSpec((None, 1, tn), lambda i, j: (layer, 0, j)),
            pl.BlockSpec((None, tn, D_MODEL), lambda i, j: (layer, j, 0)),
            pl.BlockSpec((1, D_MODEL), lambda i, j: (0, 0)),
        ],
        out_specs=pl.BlockSpec((tm, D_MODEL), lambda i, j: (i, 0)),
        out_shape=jax.ShapeDtypeStruct((m, D_MODEL), F32),
        scratch_shapes=[pltpu.VMEM((tm + BF16_SUBLANES, D_MODEL), BF16),
                        pltpu.VMEM((tm, D_MODEL), F32)],
        compiler_params=_cparams(("parallel", "arbitrary")),
        name="conv_ffn",
    )(x, x, nw, w_up, w_up, conv_w, conv_b, w_down, final_w)


def _gate_lanes(vals, start):
    z = jnp.zeros((vals.shape[0], 1, LANES), F32)
    return z.at[:, 0, start:start + vals.shape[1]].set(vals)


def kernel(x, ev_norm_w, ev_w_in, dn_conv_w, dn_a_log, dn_dt_bias, dn_norm_w, fox_f_bias, ev_w_out,
           od_norm_w, od_w_in, diff_lambda, diff_subln_w, od_w_out,
           ffn_norm_w, ffn_w_up, ffn_conv_w, ffn_conv_b, ffn_w_down, final_norm_w):
    batch, seq, _ = x.shape
    depth = ffn_norm_w.shape[0]
    h = x.reshape(batch * seq, D_MODEL)

    fox_v0 = EV_FOX0 + 2 * H_FOX * D_FOX
    fox_v1 = fox_v0 + H_FOX * D_FOX
    ev_w_in_b = ev_w_in.astype(BF16)
    ev_w_fox = ev_w_in[:, :, EV_FOX0:fox_v0].astype(BF16)
    ev_w_vt = ev_w_in[:, :, fox_v0:fox_v1].transpose(0, 2, 1).astype(BF16)
    ev_w_small = jnp.concatenate(
        [ev_w_in[:, :, EV_GATE0:EV_FOX0], ev_w_in[:, :, fox_v1:],
         jnp.zeros((ev_w_in.shape[0], D_MODEL, LANES - 2 * H_DN - H_FOX), F32)], axis=2).astype(BF16)
    ev_w_out_b = ev_w_out.astype(BF16)
    od_w_in_b = od_w_in.astype(BF16)
    od_w_vt = od_w_in[:, :, 2 * D_MODEL:].transpose(0, 2, 1).astype(BF16)
    od_w_out_b = od_w_out.astype(BF16)
    ffn_w_up_b = ffn_w_up.astype(BF16)
    ffn_w_down_b = ffn_w_down.astype(BF16)
    nega = _gate_lanes(-jnp.exp(dn_a_log), _G_DECAY)
    dtb = _gate_lanes(dn_dt_bias, _G_DECAY)
    fb = _gate_lanes(fox_f_bias, _G_FOX)
    cos_t, sin_t = _rope_tables(seq)
    row3 = lambda a: a[:, None, :]

    for i in range(depth):
        j = i // 2
        if i % 2 == 0:
            big, vt, small = _ev_proj(h, row3(ev_norm_w), ev_w_in_b, ev_w_fox, ev_w_vt, ev_w_small, j)
            gcol, grow, cb = _gates(small, nega[j], dtb[j], fb[j], batch=batch, seq=seq)
            grow4 = grow.reshape(batch, 16, seq // DN_CHUNK, DN_CHUNK).transpose(0, 2, 1, 3)
            o_d = _deltanet(big, dn_conv_w, gcol, grow4, row3(dn_norm_w), j, batch=batch, seq=seq)
            o_f = _fox(big, vt, cb, batch=batch, seq=seq)
            h = _out_proj(h, [o_d, o_f], ev_w_out_b, j)
        else:
            lambda_init = 0.8 - 0.6 * math.exp(-0.3 * i)
            qk, vt = _od_proj(h, row3(od_norm_w), od_w_in_b, od_w_vt, cos_t, sin_t, j, seq=seq)
            o = _diff_attn(qk, vt, diff_lambda, row3(diff_subln_w), j, lambda_init, batch=batch, seq=seq)
            h = _out_proj(h, [o], od_w_out_b, j)
        h = _ffn(h, row3(ffn_norm_w), ffn_w_up_b, ffn_conv_w, row3(ffn_conv_b), ffn_w_down_b,
                 final_norm_w[None, :], i, seq=seq, final_norm=(i == depth - 1))
    return h.reshape(batch, seq, D_MODEL)
```

```python
import functools
import math

import jax
import jax.numpy as jnp
from jax import lax
from jax.experimental import pallas as pl
from jax.experimental.pallas import tpu as pltpu

F32 = jnp.float32
BF16 = jnp.bfloat16

D_MODEL = 1024
H_DN = 4
DK_DN = 128
DV_DN = 128
DN_CONV = 4
DN_CHUNK = 64
H_FOX = 4
D_FOX = 128
D_DIFF = 64
H_DIFF = D_MODEL // (2 * D_DIFF)
ROPE_THETA = 10000.0
D_FF = 128 * ((8 * D_MODEL // 3 + 127) // 128)
FFN_CONV = 3
EPS = 1e-6
LOG2E = 1.4426950408889634

LANES = 128
MXU_N = 256
BF16_SUBLANES = 16
VMEM_LIMIT = 56 * 1024 * 1024
VT_ROWS = LANES + BF16_SUBLANES
CB_TERMS = 3

DN_QKV = H_DN * (2 * DK_DN + DV_DN)
EV_GATE0 = DN_QKV + H_DN * DV_DN
EV_FOX0 = EV_GATE0 + 2 * H_DN
EV_BIG = EV_GATE0 + 2 * H_FOX * D_FOX
_QF, _KF = EV_GATE0 // LANES, EV_GATE0 // LANES + H_FOX
_G_BETA, _G_DECAY, _G_FOX = 0, 4, 8


def _cparams(sem):
    return pltpu.CompilerParams(dimension_semantics=sem, vmem_limit_bytes=VMEM_LIMIT)


def _resident(block_shape, index_map):
    return pl.BlockSpec(block_shape, index_map, pipeline_mode=pl.Buffered(1))


def _rms(x, w):
    return x * lax.rsqrt(jnp.mean(x * x, axis=-1, keepdims=True) + EPS) * w


def _sigmoid(x):
    return 1.0 / (1.0 + jnp.exp(-x))


def _silu(x):
    return x * _sigmoid(x)


def _dot(a, b):
    return jnp.dot(a, b, preferred_element_type=F32)


def _dot_nt(a, b):
    return lax.dot_general(a, b, (((1,), (1,)), ((), ())), preferred_element_type=F32)


def _store_vt(vt_ref, vt, heads):
    ones = jnp.ones((VT_ROWS - LANES, vt.shape[1]), BF16)
    for h in range(heads):
        vt_ref[h * VT_ROWS:h * VT_ROWS + LANES, :] = vt[h * LANES:(h + 1) * LANES].astype(BF16)
        vt_ref[h * VT_ROWS + LANES:(h + 1) * VT_ROWS, :] = ones


def _ev_proj_kernel(x_ref, nw_ref, wa_ref, wf_ref, wvt_ref, ws_ref, big_ref, vt_ref, small_ref,
                    hn_ref, *, tn):
    hn_ref[...] = _rms(x_ref[...], nw_ref[...]).astype(BF16)
    small_ref[...] = _dot(hn_ref[...], ws_ref[...])
    for c in range(EV_GATE0 // tn):
        cols = slice(c * tn, (c + 1) * tn)
        big_ref[:, cols] = _dot(hn_ref[...], wa_ref[:, cols]).astype(BF16)
    for c in range(wf_ref.shape[1] // tn):
        y = _dot(hn_ref[...], wf_ref[:, c * tn:(c + 1) * tn])
        if (c + 1) * tn <= H_FOX * D_FOX:
            y = y * ((D_FOX ** -0.5) * LOG2E)
        big_ref[:, EV_GATE0 + c * tn:EV_GATE0 + (c + 1) * tn] = y.astype(BF16)
    _store_vt(vt_ref, _dot_nt(wvt_ref[...], hn_ref[...]), H_FOX)


def _ev_proj(x, nw, w_in, w_fox, w_vt, w_small, layer, *, tm=512, tn=512):
    m = x.shape[0]
    return pl.pallas_call(
        functools.partial(_ev_proj_kernel, tn=tn),
        grid=(m // tm,),
        in_specs=[
            pl.BlockSpec((tm, D_MODEL), lambda i: (i, 0)),
            _resident((None, 1, D_MODEL), lambda i: (layer, 0, 0)),
            _resident((None, D_MODEL, EV_GATE0), lambda i: (layer, 0, 0)),
            _resident((None, D_MODEL, w_fox.shape[2]), lambda i: (layer, 0, 0)),
            _resident((None, w_vt.shape[1], D_MODEL), lambda i: (layer, 0, 0)),
            _resident((None, D_MODEL, LANES), lambda i: (layer, 0, 0)),
        ],
        out_specs=[
            pl.BlockSpec((tm, EV_BIG), lambda i: (i, 0)),
            pl.BlockSpec((H_FOX * VT_ROWS, tm), lambda i: (0, i)),
            pl.BlockSpec((tm, LANES), lambda i: (i, 0)),
        ],
        out_shape=[
            jax.ShapeDtypeStruct((m, EV_BIG), BF16),
            jax.ShapeDtypeStruct((H_FOX * VT_ROWS, m), BF16),
            jax.ShapeDtypeStruct((m, LANES), F32),
        ],
        scratch_shapes=[pltpu.VMEM((tm, D_MODEL), BF16)],
        compiler_params=_cparams(("parallel",)),
        name="ev_proj",
    )(x, nw, w_in, w_fox, w_vt, w_small)


def _gate_kernel(lg_ref, nega_ref, dtb_ref, fb_ref, col_ref, row_ref, kx_ref, *, seq):
    t = lg_ref[...]
    lane = lax.broadcasted_iota(jnp.int32, t.shape, 1)
    row = lax.broadcasted_iota(jnp.int32, t.shape, 0)
    is_decay = (lane >= _G_DECAY) & (lane < _G_FOX)
    is_fox = (lane >= _G_FOX) & (lane < _G_FOX + H_FOX)
    beta = _sigmoid(t)
    td = t + dtb_ref[...]
    softplus = jnp.maximum(td, 0.0) + jnp.log(1.0 + jnp.exp(-jnp.abs(td)))
    g = nega_ref[...] * softplus
    tf = t + fb_ref[...]
    logf = (jnp.minimum(tf, 0.0) - jnp.log(1.0 + jnp.exp(-jnp.abs(tf)))) * LOG2E
    v = jnp.where(is_decay, g, jnp.where(is_fox, logf, 0.0))
    pos = jnp.where(is_decay, row % DN_CHUNK, jnp.where(is_fox, row, 0))
    k = 1
    while k < seq:
        shifted = pltpu.roll(v, k, 0)
        v = v + jnp.where(pos >= k, shifted, 0.0)
        k *= 2
    out = jnp.where(lane < _G_DECAY, beta, v)
    col_ref[...] = out
    row_ref[...] = out.T[:16, :]
    for h in range(H_FOX):
        rest = jnp.broadcast_to(v[:, _G_FOX + h:_G_FOX + h + 1], t.shape)
        ext = jnp.zeros(t.shape, F32)
        for term in range(CB_TERMS):
            part = rest.astype(BF16).astype(F32)
            ext = jnp.where(lane == term, part, ext)
            rest = rest - part
        kx_ref[h] = ext.astype(BF16)


def _gates(small, nega, dtb, fb, *, batch, seq):
    return pl.pallas_call(
        functools.partial(_gate_kernel, seq=seq),
        grid=(batch,),
        in_specs=[
            pl.BlockSpec((seq, LANES), lambda b: (b, 0)),
            pl.BlockSpec((1, LANES), lambda b: (0, 0)),
            pl.BlockSpec((1, LANES), lambda b: (0, 0)),
            pl.BlockSpec((1, LANES), lambda b: (0, 0)),
        ],
        out_specs=[
            pl.BlockSpec((seq, LANES), lambda b: (b, 0)),
            pl.BlockSpec((None, 16, seq), lambda b: (b, 0, 0)),
            pl.BlockSpec((None, H_FOX, seq, LANES), lambda b: (b, 0, 0, 0)),
        ],
        out_shape=[
            jax.ShapeDtypeStruct((batch * seq, LANES), F32),
            jax.ShapeDtypeStruct((batch, 16, seq), F32),
            jax.ShapeDtypeStruct((batch, H_FOX, seq, LANES), BF16),
        ],
        compiler_params=_cparams(("parallel",)),
        name="gates",
    )(small, nega, dtb, fb)


def _dn_kernel(q_ref, k_ref, v_ref, qh_ref, kh_ref, vh_ref, z_ref, cq_ref, ck_ref, cv_ref,
               gcol_ref, grow_ref, nw_ref, o_ref, state_ref, *, ts):
    si = pl.program_id(1)
    c = DN_CHUNK
    nc = ts // c
    halo = BF16_SUBLANES
    heads = range(H_DN)

    @pl.when(si == 0)
    def _():
        state_ref[...] = jnp.zeros_like(state_ref)

    def conv_silu(x_ref, halo_ref, cw_ref):
        x = x_ref[...].astype(F32)
        prev = jnp.where(si == 0, 0.0, halo_ref[...].astype(F32))
        xe = jnp.concatenate([prev, x], axis=0)
        cw = cw_ref[...]
        y = cw[DN_CONV - 1:DN_CONV] * x
        for j in range(DN_CONV - 1):
            off = halo - (DN_CONV - 1) + j
            y = y + cw[j:j + 1] * xe[off:off + ts]
        return _silu(y)

    def l2n(x):
        return x * lax.rsqrt(jnp.sum(x * x, axis=-1, keepdims=True) + EPS)

    qa = conv_silu(q_ref, qh_ref, cq_ref)
    ka = conv_silu(k_ref, kh_ref, ck_ref)
    va = conv_silu(v_ref, vh_ref, cv_ref)
    gates = gcol_ref[...]
    lane = lax.broadcasted_iota(jnp.int32, gates.shape, 1)

    r2 = lax.broadcasted_iota(jnp.int32, (c, c), 0)
    c2 = lax.broadcasted_iota(jnp.int32, (c, c), 1)
    incl = r2 >= c2
    incl_f = incl.astype(F32)
    strict_f = (r2 > c2).astype(F32)

    def level_mask(s):
        hit = ((r2 // (2 * s)) == (c2 // (2 * s))) & ((r2 // s) % 2 == 1) & ((c2 // s) % 2 == 0)
        return hit.astype(F32)

    bdot_nt = lambda a, b: jnp.einsum('cid,cjd->cij', a, b, preferred_element_type=F32)
    bdot = lambda a, b: jnp.einsum('cij,cjd->cid', a, b, preferred_element_type=F32)
    chunks = lambda a: a.reshape(nc, c, a.shape[-1])

    k3, kb3, q3, rhs, qg3, kd3, egl, decay = [], [], [], [], [], [], [], []
    for h in heads:
        cols = slice(h * DK_DN, (h + 1) * DK_DN)
        q = l2n(qa[:, cols]) * (DK_DN ** -0.5)
        k = l2n(ka[:, cols])
        v = va[:, cols]
        beta = jnp.sum(jnp.where(lane == _G_BETA + h, gates, 0.0), axis=1, keepdims=True)
        gc = jnp.sum(jnp.where(lane == _G_DECAY + h, gates, 0.0), axis=1, keepdims=True)
        g_row = grow_ref[:, _G_DECAY + h:_G_DECAY + h + 1, :]
        gc3 = chunks(gc)
        gl3 = gc3[:, c - 1:c, :]
        eg = jnp.exp(gc)
        kb = k * beta
        k3.append(chunks(k).astype(BF16))
        kb3.append(chunks(kb).astype(BF16))
        q3.append(chunks(q).astype(BF16))
        rhs.append(jnp.concatenate([chunks(v * beta), chunks(kb * eg)], axis=2))
        qg3.append(chunks(q * eg).astype(BF16))
        kd3.append((chunks(k) * jnp.exp(gl3 - gc3)).astype(BF16))
        egl.append(jnp.exp(gl3))
        decay.append(jnp.exp(jnp.where(incl[None], gc3 - g_row, 0.0)))

    lower = [bdot_nt(kb3[h], k3[h]) * (decay[h] * strict_f[None]) for h in heads]
    qk = [(bdot_nt(q3[h], k3[h]) * (decay[h] * incl_f[None])).astype(BF16) for h in heads]

    first = level_mask(1)
    xs = [-(lower[h] * first[None]) for h in heads]
    s = 2
    while s < c:
        lm = level_mask(s)
        cs = [lower[h] * lm[None] for h in heads]
        y = [cs[h] + bdot(cs[h].astype(BF16), xs[h].astype(BF16)) for h in heads]
        xs = [xs[h] - y[h] - bdot(xs[h].astype(BF16), y[h].astype(BF16)) for h in heads]
        s *= 2

    uw = [rhs[h] + bdot(xs[h].astype(BF16), rhs[h].astype(BF16)) for h in heads]
    w3 = [uw[h][:, :, DV_DN:].astype(BF16) for h in heads]

    state = [state_ref[h] for h in heads]
    outs = [[] for _ in heads]
    for n in range(nc):
        ws = [_dot(jnp.concatenate([w3[h][n], qg3[h][n]], axis=0), state[h].astype(BF16))
              for h in heads]
        vnb = [(uw[h][n, :, :DV_DN] - ws[h][:c]).astype(BF16) for h in heads]
        for h in heads:
            outs[h].append(ws[h][c:] + _dot(qk[h][n], vnb[h]))
        state = [state[h] * egl[h][n] + lax.dot_general(kd3[h][n], vnb[h], (((0,), (0,)), ((), ())),
                                                        preferred_element_type=F32)
                 for h in heads]
    for h in heads:
        cols = slice(h * DV_DN, (h + 1) * DV_DN)
        state_ref[h] = state[h]
        o = jnp.concatenate(outs[h], axis=0)
        o = _rms(o, nw_ref[...]) * _silu(z_ref[:, cols].astype(F32))
        o_ref[:, cols] = o.astype(BF16)


def _deltanet(big, conv_w, gcol, grow4, norm_w, layer, *, batch, seq, ts=512):
    m = batch * seq
    ns = seq // ts
    nc = ts // DN_CHUNK
    hb = ts // BF16_SUBLANES
    w = H_DN * DK_DN

    def blk(col):
        return pl.BlockSpec((ts, w), lambda b, s: (b * ns + s, col))

    def halo(col):
        return pl.BlockSpec((BF16_SUBLANES, w),
                            lambda b, s: (jnp.maximum((b * ns + s) * hb - 1, 0), col))

    def cw(col):
        return pl.BlockSpec((None, DN_CONV, w), lambda b, s: (layer, 0, col))

    return pl.pallas_call(
        functools.partial(_dn_kernel, ts=ts),
        grid=(batch, ns),
        in_specs=[
            blk(0), blk(1), blk(2), halo(0), halo(1), halo(2), blk(3),
            cw(0), cw(1), cw(2),
            pl.BlockSpec((ts, LANES), lambda b, s: (b * ns + s, 0)),
            pl.BlockSpec((None, nc, 16, DN_CHUNK), lambda b, s: (b, s, 0, 0)),
            pl.BlockSpec((None, 1, DV_DN), lambda b, s: (layer, 0, 0)),
        ],
        out_specs=pl.BlockSpec((ts, w), lambda b, s: (b * ns + s, 0)),
        out_shape=jax.ShapeDtypeStruct((m, H_DN * DV_DN), BF16),
        scratch_shapes=[pltpu.VMEM((H_DN, DK_DN, DV_DN), F32)],
        compiler_params=_cparams(("parallel", "arbitrary")),
        name="deltanet",
    )(big, big, big, big, big, big, big, conv_w, conv_w, conv_w, gcol, grow4, norm_w)


def _attn_kernel(*refs, tq, ck, diff, lambda_init):
    if diff:
        q_ref, k_ref, vt_ref, lv_ref, sw_ref, o_ref, qz_ref, m_ref, acc_ref = refs
        rsub = MXU_N // 2
    else:
        q_ref, k_ref, kx_ref, vt_ref, o_ref, qz_ref, m_ref, acc_ref = refs
        rsub = MXU_N
    nsub = tq // rsub
    ahead = 5
    kpi = 4
    qi = pl.program_id(2)

    for r in range(nsub):
        rows = q_ref[r * rsub:(r + 1) * rsub, :]
        lane = lax.broadcasted_iota(jnp.int32, rows.shape, 1)
        if diff:
            qz_ref[r, :rsub, :] = jnp.where(lane < D_DIFF, rows, jnp.zeros_like(rows))
            qz_ref[r, rsub:, :] = jnp.where(lane >= D_DIFF, rows, jnp.zeros_like(rows))
        else:
            qz_ref[r, :, :LANES] = rows
            qz_ref[r, :, LANES:] = jnp.where(lane < CB_TERMS, -1.0, 0.0).astype(BF16)
    m_ref[...] = jnp.full_like(m_ref, -jnp.inf)
    acc_ref[...] = jnp.zeros_like(acc_ref)

    def run(tiles):
        loaded = {}

        def operands(t):
            if t not in loaded:
                ks = tiles[t][0]
                k = k_ref[pl.ds(ks, ck), :]
                if not diff:
                    k = jnp.concatenate([k, kx_ref[pl.ds(ks, ck), :]], axis=1)
                loaded[t] = (k, vt_ref[:, pl.ds(ks, ck)])
            return loaded[t]

        items = [(t, r) for t, (_, plan) in enumerate(tiles) for r in range(nsub)
                 if plan[r] != "skip"]
        scores = {}

        def issue(item):
            scores[item] = _dot_nt(operands(item[0])[0], qz_ref[item[1]])

        for item in items[:ahead]:
            issue(item)
        for idx, (t, r) in enumerate(items):
            if idx + ahead < len(items):
                issue(items[idx + ahead])
            s = scores.pop((t, r))
            vt = operands(t)[1]
            plan = tiles[t][1]
            if plan[r] is not None:
                kpos = lax.broadcasted_iota(jnp.int32, s.shape, 0) + plan[r]
                qpos = lax.broadcasted_iota(jnp.int32, s.shape, 1)
                if diff:
                    qpos = qpos % rsub
                s = jnp.where(kpos <= qpos + r * rsub, s, -jnp.inf)
            m_prev = m_ref[r]
            m_new = jnp.maximum(m_prev, jnp.max(s, axis=0, keepdims=True))
            p = jnp.exp2(s - m_new)
            acc_ref[r] = jnp.exp2(m_prev - m_new) * acc_ref[r] + _dot(vt, p.astype(BF16))
            m_ref[r] = m_new

    def body(t, carry):
        base = t * (kpi * ck)
        run([(pl.multiple_of(base + u * ck, ck), [None] * nsub) for u in range(kpi)])
        return carry

    lax.fori_loop(0, qi * (tq // (kpi * ck)), body, 0)
    diag = []
    for jd in range(tq // ck):
        plan = []
        for r in range(nsub):
            if jd * ck > r * rsub + rsub - 1:
                plan.append("skip")
            elif jd * ck + ck - 1 <= r * rsub:
                plan.append(None)
            else:
                plan.append(jd * ck)
        diag.append((pl.multiple_of(qi * tq + jd * ck, ck), plan))
    run(diag)

    if diff:
        lv = lv_ref[...]
        lam = (jnp.exp(jnp.sum(lv[0:1] * lv[1:2], axis=1, keepdims=True))
               - jnp.exp(jnp.sum(lv[2:3] * lv[3:4], axis=1, keepdims=True)) + lambda_init)
    for r in range(nsub):
        tot = acc_ref[r]
        ot = tot[:LANES] / tot[LANES:LANES + 1]
        if diff:
            o = (ot[:, :rsub] - lam * ot[:, rsub:]).T
            o = _rms(o, sw_ref[...]) * (1.0 - lambda_init)
        else:
            o = ot.T
        o_ref[r * rsub:(r + 1) * rsub, :] = o.astype(BF16)


def _attn_scratch(tq, rsub, kdim):
    nsub = tq // rsub
    return [pltpu.VMEM((nsub, MXU_N, kdim), BF16), pltpu.VMEM((nsub, 1, MXU_N), F32),
            pltpu.VMEM((nsub, VT_ROWS, MXU_N), F32)]


def _fox(big, kx, vt, *, batch, seq, tq=1024, ck=256):
    nq = seq // tq
    big3 = big.reshape(batch, seq, EV_BIG)
    return pl.pallas_call(
        functools.partial(_attn_kernel, tq=tq, ck=ck, diff=False, lambda_init=0.0),
        grid=(batch, H_FOX, nq),
        in_specs=[
            pl.BlockSpec((None, tq, LANES), lambda b, h, i: (b, i, _QF + h)),
            pl.BlockSpec((None, seq, LANES), lambda b, h, i: (b, 0, _KF + h)),
            pl.BlockSpec((None, None, seq, LANES), lambda b, h, i: (b, h, 0, 0)),
            pl.BlockSpec((VT_ROWS, seq), lambda b, h, i: (h, b)),
        ],
        out_specs=pl.BlockSpec((None, tq, LANES), lambda b, h, i: (b, i, h)),
        out_shape=jax.ShapeDtypeStruct((batch, seq, H_FOX * D_FOX), BF16),
        scratch_shapes=_attn_scratch(tq, MXU_N, 2 * LANES),
        compiler_params=_cparams(("parallel", "parallel", "arbitrary")),
        name="fox_attn",
    )(big3, big3, kx, vt).reshape(batch * seq, H_FOX * D_FOX)


def _diff_attn(qk, vt, lam_vecs, subln_w, layer, lambda_init, *, batch, seq, tq=1024, ck=256):
    nq = seq // tq
    qk3 = qk.reshape(batch, seq, 2 * D_MODEL)
    return pl.pallas_call(
        functools.partial(_attn_kernel, tq=tq, ck=ck, diff=True, lambda_init=lambda_init),
        grid=(batch, H_DIFF, nq),
        in_specs=[
            pl.BlockSpec((None, tq, LANES), lambda b, h, i: (b, i, h)),
            pl.BlockSpec((None, seq, LANES), lambda b, h, i: (b, 0, H_DIFF + h)),
            pl.BlockSpec((VT_ROWS, seq), lambda b, h, i: (h, b)),
            pl.BlockSpec((None, 4, D_DIFF), lambda b, h, i: (layer, 0, 0)),
            pl.BlockSpec((None, 1, 2 * D_DIFF), lambda b, h, i: (layer, 0, 0)),
        ],
        out_specs=pl.BlockSpec((None, tq, LANES), lambda b, h, i: (b, i, h)),
        out_shape=jax.ShapeDtypeStruct((batch, seq, D_MODEL), BF16),
        scratch_shapes=_attn_scratch(tq, MXU_N // 2, LANES),
        compiler_params=_cparams(("parallel", "parallel", "arbitrary")),
        name="diff_attn",
    )(qk3, qk3, vt, lam_vecs, subln_w).reshape(batch * seq, D_MODEL)


def _od_proj_kernel(x_ref, nw_ref, w_ref, wvt_ref, cos_ref, sin_ref, o_ref, vt_ref, hn_ref, *, tn):
    hn_ref[...] = _rms(x_ref[...], nw_ref[...]).astype(BF16)
    half = D_DIFF // 2
    lane = lax.broadcasted_iota(jnp.int32, (x_ref.shape[0], LANES), 1)
    first_half = (lane % D_DIFF) < half
    for c in range(2 * D_MODEL // tn):
        y = _dot(hn_ref[...], w_ref[:, c * tn:(c + 1) * tn])
        table = (c * tn) // D_MODEL
        cos = cos_ref[table]
        sin = sin_ref[table]
        for g in range(tn // LANES):
            yg = y[:, g * LANES:(g + 1) * LANES]
            rot = jnp.where(first_half, pltpu.roll(yg, LANES - half, 1), pltpu.roll(yg, half, 1))
            o_ref[:, c * tn + g * LANES:c * tn + (g + 1) * LANES] = (yg * cos + rot * sin).astype(BF16)
    _store_vt(vt_ref, _dot_nt(wvt_ref[...], hn_ref[...]), H_DIFF)


def _od_proj(x, nw, w, w_vt, cos_t, sin_t, layer, *, seq, tm=512, tn=512):
    m = x.shape[0]
    sb = seq // tm
    return pl.pallas_call(
        functools.partial(_od_proj_kernel, tn=tn),
        grid=(m // tm,),
        in_specs=[
            pl.BlockSpec((tm, D_MODEL), lambda i: (i, 0)),
            _resident((None, 1, D_MODEL), lambda i: (layer, 0, 0)),
            _resident((None, D_MODEL, 2 * D_MODEL), lambda i: (layer, 0, 0)),
            _resident((None, D_MODEL, D_MODEL), lambda i: (layer, 0, 0)),
            pl.BlockSpec((2, tm, LANES), lambda i: (0, i % sb, 0)),
            pl.BlockSpec((2, tm, LANES), lambda i: (0, i % sb, 0)),
        ],
        out_specs=[
            pl.BlockSpec((tm, 2 * D_MODEL), lambda i: (i, 0)),
            pl.BlockSpec((H_DIFF * VT_ROWS, tm), lambda i: (0, i)),
        ],
        out_shape=[
            jax.ShapeDtypeStruct((m, 2 * D_MODEL), BF16),
            jax.ShapeDtypeStruct((H_DIFF * VT_ROWS, m), BF16),
        ],
        scratch_shapes=[pltpu.VMEM((tm, D_MODEL), BF16)],
        compiler_params=_cparams(("parallel",)),
        name="od_proj",
    )(x, nw, w, w_vt, cos_t, sin_t)


def _rope_tables(seq):
    half = D_DIFF // 2
    inv = 1.0 / (ROPE_THETA ** (jnp.arange(half, dtype=F32) * 2.0 / D_DIFF))
    ang = jnp.arange(seq, dtype=F32)[:, None] * inv[None, :]
    cos = jnp.tile(jnp.cos(ang), (1, LANES // half))
    sin = jnp.tile(jnp.concatenate([-jnp.sin(ang), jnp.sin(ang)], axis=1), (1, LANES // D_DIFF))
    qs = (D_DIFF ** -0.5) * LOG2E
    return jnp.stack([cos * qs, cos]), jnp.stack([sin * qs, sin])


def _ffn_kernel(*refs, n_in, tm, tn, rows_per_seq, final_norm):
    x_ref, xp_ref = refs[:2]
    a_refs = refs[2:2 + n_in]
    ap_refs = refs[2 + n_in:2 + 2 * n_in]
    wo_ref, nw_ref, wup_ref, cw_ref, cb_ref, wd_ref, fw_ref, o_ref, hn_ref, hmid_ref = refs[2 + 2 * n_in:]
    halo = BF16_SUBLANES
    kw = a_refs[0].shape[1]

    xm = x_ref[...]
    xpm = xp_ref[...]
    for r in range(n_in):
        wo = wo_ref[r * kw:(r + 1) * kw, :]
        xm = xm + _dot(a_refs[r][...], wo)
        xpm = xpm + _dot(ap_refs[r][...], wo)
    seq_start = (pl.program_id(0) % rows_per_seq) == 0
    hn_ref[:halo, :] = jnp.where(seq_start, 0.0, _rms(xpm, nw_ref[...])).astype(BF16)
    hn_ref[halo:, :] = _rms(xm, nw_ref[...]).astype(BF16)

    for c in range(D_FF // tn):
        cols = slice(c * tn, (c + 1) * tn)
        ge = _dot(hn_ref[...], wup_ref[:, cols])
        gate = cb_ref[:, cols] + cw_ref[FFN_CONV - 1:FFN_CONV, cols] * ge[halo:]
        for t in range(FFN_CONV - 1):
            off = halo - (FFN_CONV - 1) + t
            gate = gate + cw_ref[t:t + 1, cols] * ge[off:off + tm]
        val = _dot(hn_ref[halo:, :], wup_ref[:, D_FF + c * tn:D_FF + (c + 1) * tn])
        hmid_ref[:, cols] = (_silu(gate) * val).astype(BF16)

    y = xm + _dot(hmid_ref[...], wd_ref[...])
    if final_norm:
        y = _rms(y, fw_ref[...])
    o_ref[...] = y


def _ffn(x, acts, w_out, out_layer, nw, w_up, conv_w, conv_b, w_down, final_w, layer, *, seq,
         final_norm, tm=512, tn=256):
    m = x.shape[0]
    n_in = len(acts)
    kw = acts[0].shape[1]
    hb = tm // BF16_SUBLANES
    prev = lambda i: (jnp.maximum(i * hb - 1, 0), 0)
    in_specs = [pl.BlockSpec((tm, D_MODEL), lambda i: (i, 0)),
                pl.BlockSpec((BF16_SUBLANES, D_MODEL), prev)]
    in_specs += [pl.BlockSpec((tm, kw), lambda i: (i, 0)) for _ in acts]
    in_specs += [pl.BlockSpec((BF16_SUBLANES, kw), prev) for _ in acts]
    in_specs += [
        _resident((None, D_MODEL, D_MODEL), lambda i: (out_layer, 0, 0)),
        _resident((None, 1, D_MODEL), lambda i: (layer, 0, 0)),
        _resident((None, D_MODEL, 2 * D_FF), lambda i: (layer, 0, 0)),
        _resident((None, FFN_CONV, D_FF), lambda i: (layer, 0, 0)),
        _resident((None, 1, D_FF), lambda i: (layer, 0, 0)),
        _resident((None, D_FF, D_MODEL), lambda i: (layer, 0, 0)),
        _resident((1, D_MODEL), lambda i: (0, 0)),
    ]
    return pl.pallas_call(
        functools.partial(_ffn_kernel, n_in=n_in, tm=tm, tn=tn, rows_per_seq=seq // tm,
                          final_norm=final_norm),
        grid=(m // tm,),
        in_specs=in_specs,
        out_specs=pl.BlockSpec((tm, D_MODEL), lambda i: (i, 0)),
        out_shape=jax.ShapeDtypeStruct((m, D_MODEL), F32),
        scratch_shapes=[pltpu.VMEM((tm + BF16_SUBLANES, D_MODEL), BF16),
                        pltpu.VMEM((tm, D_FF), BF16)],
        compiler_params=_cparams(("parallel",)),
        name="conv_ffn",
    )(x, x, *acts, *acts, w_out, nw, w_up, conv_w, conv_b, w_down, final_w)


def _gate_lanes(vals, start):
    z = jnp.zeros((vals.shape[0], 1, LANES), F32)
    return z.at[:, 0, start:start + vals.shape[1]].set(vals)


def kernel(x, ev_norm_w, ev_w_in, dn_conv_w, dn_a_log, dn_dt_bias, dn_norm_w, fox_f_bias, ev_w_out,
           od_norm_w, od_w_in, diff_lambda, diff_subln_w, od_w_out,
           ffn_norm_w, ffn_w_up, ffn_conv_w, ffn_conv_b, ffn_w_down, final_norm_w):
    batch, seq, _ = x.shape
    depth = ffn_norm_w.shape[0]
    h = x.reshape(batch * seq, D_MODEL)

    fox_v0 = EV_FOX0 + 2 * H_FOX * D_FOX
    fox_v1 = fox_v0 + H_FOX * D_FOX
    ev_w_in_b = ev_w_in.astype(BF16)
    ev_w_fox = ev_w_in[:, :, EV_FOX0:fox_v0].astype(BF16)
    ev_w_vt = ev_w_in[:, :, fox_v0:fox_v1].transpose(0, 2, 1).astype(BF16)
    ev_w_small = jnp.concatenate(
        [ev_w_in[:, :, EV_GATE0:EV_FOX0], ev_w_in[:, :, fox_v1:],
         jnp.zeros((ev_w_in.shape[0], D_MODEL, LANES - 2 * H_DN - H_FOX), F32)], axis=2).astype(BF16)
    ev_w_out_b = ev_w_out.astype(BF16)
    od_w_in_b = od_w_in.astype(BF16)
    od_w_vt = od_w_in[:, :, 2 * D_MODEL:].transpose(0, 2, 1).astype(BF16)
    od_w_out_b = od_w_out.astype(BF16)
    ffn_w_up_b = ffn_w_up.astype(BF16)
    ffn_w_down_b = ffn_w_down.astype(BF16)
    nega = _gate_lanes(-jnp.exp(dn_a_log), _G_DECAY)
    dtb = _gate_lanes(dn_dt_bias, _G_DECAY)
    fb = _gate_lanes(fox_f_bias, _G_FOX)
    cos_t, sin_t = _rope_tables(seq)
    row3 = lambda a: a[:, None, :]

    for i in range(depth):
        j = i // 2
        if i % 2 == 0:
            big, vt, small = _ev_proj(h, row3(ev_norm_w), ev_w_in_b, ev_w_fox, ev_w_vt, ev_w_small, j)
            gcol, grow, kx = _gates(small, nega[j], dtb[j], fb[j], batch=batch, seq=seq)
            grow4 = grow.reshape(batch, 16, seq // DN_CHUNK, DN_CHUNK).transpose(0, 2, 1, 3)
            o_d = _deltanet(big, dn_conv_w, gcol, grow4, row3(dn_norm_w), j, batch=batch, seq=seq)
            o_f = _fox(big, kx, vt, batch=batch, seq=seq)
            acts, w_out = [o_d, o_f], ev_w_out_b
        else:
            lambda_init = 0.8 - 0.6 * math.exp(-0.3 * i)
            qk, vt = _od_proj(h, row3(od_norm_w), od_w_in_b, od_w_vt, cos_t, sin_t, j, seq=seq)
            o = _diff_attn(qk, vt, diff_lambda, row3(diff_subln_w), j, lambda_init, batch=batch, seq=seq)
            acts, w_out = [o], od_w_out_b
        h = _ffn(h, acts, w_out, j, row3(ffn_norm_w), ffn_w_up_b, ffn_conv_w, row3(ffn_conv_b),
                 ffn_w_down_b, final_norm_w[None, :], i, seq=seq, final_norm=(i == depth - 1))
    return h.reshape(batch, seq, D_MODEL)
```

```python
import functools
import math

import jax
import jax.numpy as jnp
from jax import lax
from jax.experimental import pallas as pl
from jax.experimental.pallas import tpu as pltpu

F32 = jnp.float32
BF16 = jnp.bfloat16

D_MODEL = 1024
H_DN = 4
DK_DN = 128
DV_DN = 128
DN_CONV = 4
DN_CHUNK = 64
H_FOX = 4
D_FOX = 128
D_DIFF = 64
H_DIFF = D_MODEL // (2 * D_DIFF)
ROPE_THETA = 10000.0
D_FF = 128 * ((8 * D_MODEL // 3 + 127) // 128)
FFN_CONV = 3
EPS = 1e-6
LOG2E = 1.4426950408889634

LANES = 128
MXU_N = 256
BF16_SUBLANES = 16
VMEM_LIMIT = 56 * 1024 * 1024
VT_ROWS = LANES + BF16_SUBLANES
CB_TERMS = 3
KX_STRIDE = 4
GATE_ROWS = 16

DN_QKV = H_DN * (2 * DK_DN + DV_DN)
EV_GATE0 = DN_QKV + H_DN * DV_DN
EV_FOX0 = EV_GATE0 + 2 * H_DN
EV_BIG = EV_GATE0 + 2 * H_FOX * D_FOX
_QF, _KF = EV_GATE0 // LANES, EV_GATE0 // LANES + H_FOX
_G_BETA, _G_DECAY, _G_FOX = 0, 4, 8


def _cparams(sem):
    return pltpu.CompilerParams(dimension_semantics=sem, vmem_limit_bytes=VMEM_LIMIT)


def _resident(block_shape, index_map):
    return pl.BlockSpec(block_shape, index_map, pipeline_mode=pl.Buffered(1))


def _rms(x, w):
    return x * lax.rsqrt(jnp.mean(x * x, axis=-1, keepdims=True) + EPS) * w


def _sigmoid(x):
    return 1.0 / (1.0 + jnp.exp(-x))


def _silu(x):
    return x * _sigmoid(x)


def _dot(a, b):
    return jnp.dot(a, b, preferred_element_type=F32)


def _dot_nt(a, b):
    return lax.dot_general(a, b, (((1,), (1,)), ((), ())), preferred_element_type=F32)


def _store_vt(vt_ref, v, head0):
    vt = v.astype(BF16).T
    ones = jnp.ones((VT_ROWS - LANES, vt.shape[1]), BF16)
    for h in range(vt.shape[0] // LANES):
        row0 = (head0 + h) * VT_ROWS
        vt_ref[row0:row0 + LANES, :] = vt[h * LANES:(h + 1) * LANES]
        vt_ref[row0 + LANES:row0 + VT_ROWS, :] = ones


def _ev_proj_kernel(x_ref, nw_ref, wa_ref, wf_ref, ws_ref, big_ref, vt_ref, small_ref, hn_ref, *, tn):
    hn_ref[...] = _rms(x_ref[...], nw_ref[...]).astype(BF16)
    small_ref[...] = _dot(hn_ref[...], ws_ref[...])
    for c in range(EV_GATE0 // tn):
        cols = slice(c * tn, (c + 1) * tn)
        big_ref[:, cols] = _dot(hn_ref[...], wa_ref[:, cols]).astype(BF16)
    n_qk = 2 * H_FOX * D_FOX
    for c in range(n_qk // tn):
        y = _dot(hn_ref[...], wf_ref[:, c * tn:(c + 1) * tn])
        if (c + 1) * tn <= H_FOX * D_FOX:
            y = y * ((D_FOX ** -0.5) * LOG2E)
        big_ref[:, EV_GATE0 + c * tn:EV_GATE0 + (c + 1) * tn] = y.astype(BF16)
    _store_vt(vt_ref, _dot(hn_ref[...], wf_ref[:, n_qk:]), 0)


def _ev_proj(x, nw, w_a, w_fox, w_small, layer, *, tm=512, tn=512):
    m = x.shape[0]
    return pl.pallas_call(
        functools.partial(_ev_proj_kernel, tn=tn),
        grid=(m // tm,),
        in_specs=[
            pl.BlockSpec((tm, D_MODEL), lambda i: (i, 0)),
            _resident((None, 1, D_MODEL), lambda i: (layer, 0, 0)),
            _resident((None, D_MODEL, EV_GATE0), lambda i: (layer, 0, 0)),
            _resident((None, D_MODEL, w_fox.shape[2]), lambda i: (layer, 0, 0)),
            _resident((None, D_MODEL, LANES), lambda i: (layer, 0, 0)),
        ],
        out_specs=[
            pl.BlockSpec((tm, EV_BIG), lambda i: (i, 0)),
            pl.BlockSpec((H_FOX * VT_ROWS, tm), lambda i: (0, i)),
            pl.BlockSpec((tm, LANES), lambda i: (i, 0)),
        ],
        out_shape=[
            jax.ShapeDtypeStruct((m, EV_BIG), BF16),
            jax.ShapeDtypeStruct((H_FOX * VT_ROWS, m), BF16),
            jax.ShapeDtypeStruct((m, LANES), F32),
        ],
        scratch_shapes=[pltpu.VMEM((tm, D_MODEL), BF16)],
        compiler_params=_cparams(("parallel",)),
        name="ev_proj",
    )(x, nw, w_a, w_fox, w_small)


def _gate_kernel(lg_ref, nega_ref, dtb_ref, fb_ref, col_ref, row_ref, kx_ref, *, seq):
    t = lg_ref[...].T[:GATE_ROWS, :]
    ch = lax.broadcasted_iota(jnp.int32, t.shape, 0)
    pos = lax.broadcasted_iota(jnp.int32, t.shape, 1)
    is_decay = (ch >= _G_DECAY) & (ch < _G_FOX)
    is_fox = (ch >= _G_FOX) & (ch < _G_FOX + H_FOX)
    beta = _sigmoid(t)
    td = t + dtb_ref[:, :1]
    softplus = jnp.maximum(td, 0.0) + jnp.log(1.0 + jnp.exp(-jnp.abs(td)))
    g = nega_ref[:, :1] * softplus
    tf = t + fb_ref[:, :1]
    logf = (jnp.minimum(tf, 0.0) - jnp.log(1.0 + jnp.exp(-jnp.abs(tf)))) * LOG2E
    v = jnp.where(is_decay, g, jnp.where(is_fox, logf, 0.0))
    idx = jnp.where(is_decay, pos % DN_CHUNK, jnp.where(is_fox, pos, 0))
    k = 1
    while k < seq:
        shifted = pltpu.roll(v, k, 1)
        v = v + jnp.where(idx >= k, shifted, 0.0)
        k *= 2
    out = jnp.where(ch < _G_DECAY, beta, v)
    row_ref[...] = out
    pad = jnp.zeros((LANES - GATE_ROWS, seq), F32)
    col_ref[...] = jnp.concatenate([out, pad], axis=0).T
    ext = jnp.zeros(t.shape, F32)
    for h in range(H_FOX):
        rest = v[_G_FOX + h:_G_FOX + h + 1, :]
        for term in range(CB_TERMS):
            part = rest.astype(BF16).astype(F32)
            ext = jnp.where(ch == KX_STRIDE * h + term, part, ext)
            rest = rest - part
    kx_ref[...] = jnp.concatenate([ext, pad], axis=0).T.astype(BF16)


def _gates(small, nega, dtb, fb, *, batch, seq):
    return pl.pallas_call(
        functools.partial(_gate_kernel, seq=seq),
        grid=(batch,),
        in_specs=[
            pl.BlockSpec((seq, LANES), lambda b: (b, 0)),
            pl.BlockSpec((GATE_ROWS, LANES), lambda b: (0, 0)),
            pl.BlockSpec((GATE_ROWS, LANES), lambda b: (0, 0)),
            pl.BlockSpec((GATE_ROWS, LANES), lambda b: (0, 0)),
        ],
        out_specs=[
            pl.BlockSpec((seq, LANES), lambda b: (b, 0)),
            pl.BlockSpec((None, GATE_ROWS, seq), lambda b: (b, 0, 0)),
            pl.BlockSpec((None, seq, LANES), lambda b: (b, 0, 0)),
        ],
        out_shape=[
            jax.ShapeDtypeStruct((batch * seq, LANES), F32),
            jax.ShapeDtypeStruct((batch, GATE_ROWS, seq), F32),
            jax.ShapeDtypeStruct((batch, seq, LANES), BF16),
        ],
        compiler_params=_cparams(("parallel",)),
        name="gates",
    )(small, nega, dtb, fb)


def _dn_kernel(q_ref, k_ref, v_ref, qh_ref, kh_ref, vh_ref, z_ref, cq_ref, ck_ref, cv_ref,
               gcol_ref, grow_ref, nw_ref, o_ref, state_ref, *, ts):
    si = pl.program_id(1)
    c = DN_CHUNK
    nc = ts // c
    halo = BF16_SUBLANES
    heads = range(H_DN)

    @pl.when(si == 0)
    def _():
        state_ref[...] = jnp.zeros_like(state_ref)

    def conv_silu(x_ref, halo_ref, cw_ref):
        x = x_ref[...].astype(F32)
        prev = jnp.where(si == 0, 0.0, halo_ref[...].astype(F32))
        xe = jnp.concatenate([prev, x], axis=0)
        cw = cw_ref[...]
        y = cw[DN_CONV - 1:DN_CONV] * x
        for j in range(DN_CONV - 1):
            off = halo - (DN_CONV - 1) + j
            y = y + cw[j:j + 1] * xe[off:off + ts]
        return _silu(y)

    def l2n(x):
        return x * lax.rsqrt(jnp.sum(x * x, axis=-1, keepdims=True) + EPS)

    qa = conv_silu(q_ref, qh_ref, cq_ref)
    ka = conv_silu(k_ref, kh_ref, ck_ref)
    va = conv_silu(v_ref, vh_ref, cv_ref)
    gates = gcol_ref[...]
    lane = lax.broadcasted_iota(jnp.int32, gates.shape, 1)

    r2 = lax.broadcasted_iota(jnp.int32, (c, c), 0)
    c2 = lax.broadcasted_iota(jnp.int32, (c, c), 1)
    incl = r2 >= c2
    incl_f = incl.astype(F32)
    strict_f = (r2 > c2).astype(F32)

    def level_mask(s):
        hit = ((r2 // (2 * s)) == (c2 // (2 * s))) & ((r2 // s) % 2 == 1) & ((c2 // s) % 2 == 0)
        return hit.astype(F32)

    bdot_nt = lambda a, b: jnp.einsum('cid,cjd->cij', a, b, preferred_element_type=F32)
    bdot = lambda a, b: jnp.einsum('cij,cjd->cid', a, b, preferred_element_type=F32)
    chunks = lambda a: a.reshape(nc, c, a.shape[-1])

    k3, kb3, q3, rhs, qg3, kd3, egl, decay = [], [], [], [], [], [], [], []
    for h in heads:
        cols = slice(h * DK_DN, (h + 1) * DK_DN)
        q = l2n(qa[:, cols]) * (DK_DN ** -0.5)
        k = l2n(ka[:, cols])
        v = va[:, cols]
        beta = jnp.sum(jnp.where(lane == _G_BETA + h, gates, 0.0), axis=1, keepdims=True)
        gc = jnp.sum(jnp.where(lane == _G_DECAY + h, gates, 0.0), axis=1, keepdims=True)
        g_row = grow_ref[:, _G_DECAY + h:_G_DECAY + h + 1, :]
        gc3 = chunks(gc)
        gl3 = gc3[:, c - 1:c, :]
        eg = jnp.exp(gc)
        kb = k * beta
        k3.append(chunks(k).astype(BF16))
        kb3.append(chunks(kb).astype(BF16))
        q3.append(chunks(q).astype(BF16))
        rhs.append(jnp.concatenate([chunks(v * beta), chunks(kb * eg)], axis=2))
        qg3.append(chunks(q * eg).astype(BF16))
        kd3.append((chunks(k) * jnp.exp(gl3 - gc3)).astype(BF16))
        egl.append(jnp.exp(gl3))
        decay.append(jnp.exp(jnp.where(incl[None], gc3 - g_row, 0.0)))

    lower = [bdot_nt(kb3[h], k3[h]) * (decay[h] * strict_f[None]) for h in heads]
    qk = [(bdot_nt(q3[h], k3[h]) * (decay[h] * incl_f[None])).astype(BF16) for h in heads]

    first = level_mask(1)
    xs = [-(lower[h] * first[None]) for h in heads]
    s = 2
    while s < c:
        lm = level_mask(s)
        cs = [lower[h] * lm[None] for h in heads]
        y = [cs[h] + bdot(cs[h].astype(BF16), xs[h].astype(BF16)) for h in heads]
        xs = [xs[h] - y[h] - bdot(xs[h].astype(BF16), y[h].astype(BF16)) for h in heads]
        s *= 2

    uw = [rhs[h] + bdot(xs[h].astype(BF16), rhs[h].astype(BF16)) for h in heads]
    w3 = [uw[h][:, :, DV_DN:].astype(BF16) for h in heads]

    state = [state_ref[h] for h in heads]
    outs = [[] for _ in heads]
    for n in range(nc):
        ws = [_dot(jnp.concatenate([w3[h][n], qg3[h][n]], axis=0), state[h].astype(BF16))
              for h in heads]
        vnb = [(uw[h][n, :, :DV_DN] - ws[h][:c]).astype(BF16) for h in heads]
        for h in heads:
            outs[h].append(ws[h][c:] + _dot(qk[h][n], vnb[h]))
        state = [state[h] * egl[h][n] + lax.dot_general(kd3[h][n], vnb[h], (((0,), (0,)), ((), ())),
                                                        preferred_element_type=F32)
                 for h in heads]
    for h in heads:
        cols = slice(h * DV_DN, (h + 1) * DV_DN)
        state_ref[h] = state[h]
        o = jnp.concatenate(outs[h], axis=0)
        o = _rms(o, nw_ref[...]) * _silu(z_ref[:, cols].astype(F32))
        o_ref[:, cols] = o.astype(BF16)


def _deltanet(big, conv_w, gcol, grow4, norm_w, layer, *, batch, seq, ts=512):
    m = batch * seq
    ns = seq // ts
    nc = ts // DN_CHUNK
    hb = ts // BF16_SUBLANES
    w = H_DN * DK_DN

    def blk(col):
        return pl.BlockSpec((ts, w), lambda b, s: (b * ns + s, col))

    def halo(col):
        return pl.BlockSpec((BF16_SUBLANES, w),
                            lambda b, s: (jnp.maximum((b * ns + s) * hb - 1, 0), col))

    def cw(col):
        return pl.BlockSpec((None, DN_CONV, w), lambda b, s: (layer, 0, col))

    return pl.pallas_call(
        functools.partial(_dn_kernel, ts=ts),
        grid=(batch, ns),
        in_specs=[
            blk(0), blk(1), blk(2), halo(0), halo(1), halo(2), blk(3),
            cw(0), cw(1), cw(2),
            pl.BlockSpec((ts, LANES), lambda b, s: (b * ns + s, 0)),
            pl.BlockSpec((None, nc, 16, DN_CHUNK), lambda b, s: (b, s, 0, 0)),
            pl.BlockSpec((None, 1, DV_DN), lambda b, s: (layer, 0, 0)),
        ],
        out_specs=pl.BlockSpec((ts, w), lambda b, s: (b * ns + s, 0)),
        out_shape=jax.ShapeDtypeStruct((m, H_DN * DV_DN), BF16),
        scratch_shapes=[pltpu.VMEM((H_DN, DK_DN, DV_DN), F32)],
        compiler_params=_cparams(("parallel", "arbitrary")),
        name="deltanet",
    )(big, big, big, big, big, big, big, conv_w, conv_w, conv_w, gcol, grow4, norm_w)


def _attn_kernel(*refs, tq, ck, diff, lambda_init):
    if diff:
        q_ref, k_ref, vt_ref, lv_ref, sw_ref, o_ref, qz_ref, m_ref, acc_ref = refs
        rsub = MXU_N // 2
    else:
        q_ref, k_ref, kx_ref, vt_ref, o_ref, qz_ref, m_ref, acc_ref = refs
        rsub = MXU_N
    nsub = tq // rsub
    ahead = 5
    kpi = 4 if diff else 8
    qi = pl.program_id(2)

    for r in range(nsub):
        rows = q_ref[r * rsub:(r + 1) * rsub, :]
        lane = lax.broadcasted_iota(jnp.int32, rows.shape, 1)
        if diff:
            qz_ref[r, :rsub, :] = jnp.where(lane < D_DIFF, rows, jnp.zeros_like(rows))
            qz_ref[r, rsub:, :] = jnp.where(lane >= D_DIFF, rows, jnp.zeros_like(rows))
        else:
            qz_ref[r, :, :LANES] = rows
            lane0 = KX_STRIDE * pl.program_id(1)
            own = (lane >= lane0) & (lane < lane0 + CB_TERMS)
            qz_ref[r, :, LANES:] = jnp.where(own, -1.0, 0.0).astype(BF16)
    m_ref[...] = jnp.full_like(m_ref, -jnp.inf)
    acc_ref[...] = jnp.zeros_like(acc_ref)

    def run(tiles):
        loaded = {}

        def operands(t):
            if t not in loaded:
                ks = tiles[t][0]
                k = k_ref[pl.ds(ks, ck), :]
                if not diff:
                    k = jnp.concatenate([k, kx_ref[pl.ds(ks, ck), :]], axis=1)
                loaded[t] = (k, vt_ref[:, pl.ds(ks, ck)])
            return loaded[t]

        items = [(t, r) for t, (_, plan) in enumerate(tiles) for r in range(nsub)
                 if plan[r] != "skip"]
        scores = {}

        def issue(item):
            scores[item] = _dot_nt(operands(item[0])[0], qz_ref[item[1]])

        for item in items[:ahead]:
            issue(item)
        for idx, (t, r) in enumerate(items):
            if idx + ahead < len(items):
                issue(items[idx + ahead])
            s = scores.pop((t, r))
            vt = operands(t)[1]
            plan = tiles[t][1]
            if plan[r] is not None:
                kpos = lax.broadcasted_iota(jnp.int32, s.shape, 0) + plan[r]
                qpos = lax.broadcasted_iota(jnp.int32, s.shape, 1)
                if diff:
                    qpos = qpos % rsub
                s = jnp.where(kpos <= qpos + r * rsub, s, -jnp.inf)
            m_prev = m_ref[r]
            m_new = jnp.maximum(m_prev, jnp.max(s, axis=0, keepdims=True))
            p = jnp.exp2(s - m_new)
            acc_ref[r] = jnp.exp2(m_prev - m_new) * acc_ref[r] + _dot(vt, p.astype(BF16))
            m_ref[r] = m_new

    def body(t, carry):
        base = t * (kpi * ck)
        run([(pl.multiple_of(base + u * ck, ck), [None] * nsub) for u in range(kpi)])
        return carry

    lax.fori_loop(0, qi * (tq // (kpi * ck)), body, 0)
    diag = []
    for jd in range(tq // ck):
        plan = []
        for r in range(nsub):
            if jd * ck > r * rsub + rsub - 1:
                plan.append("skip")
            elif jd * ck + ck - 1 <= r * rsub:
                plan.append(None)
            else:
                plan.append(jd * ck)
        diag.append((pl.multiple_of(qi * tq + jd * ck, ck), plan))
    run(diag)

    if diff:
        lv = lv_ref[...]
        lam = (jnp.exp(jnp.sum(lv[0:1] * lv[1:2], axis=1, keepdims=True))
               - jnp.exp(jnp.sum(lv[2:3] * lv[3:4], axis=1, keepdims=True)) + lambda_init)
    for r in range(nsub):
        tot = acc_ref[r]
        ot = tot[:LANES] / tot[LANES:LANES + 1]
        if diff:
            o = (ot[:, :rsub] - lam * ot[:, rsub:]).T
            o = _rms(o, sw_ref[...]) * (1.0 - lambda_init)
        else:
            o = ot.T
        o_ref[r * rsub:(r + 1) * rsub, :] = o.astype(BF16)


def _attn_scratch(tq, rsub, kdim):
    nsub = tq // rsub
    return [pltpu.VMEM((nsub, MXU_N, kdim), BF16), pltpu.VMEM((nsub, 1, MXU_N), F32),
            pltpu.VMEM((nsub, VT_ROWS, MXU_N), F32)]


def _fox(big, kx, vt, *, batch, seq, tq=2048, ck=256):
    nq = seq // tq
    big3 = big.reshape(batch, seq, EV_BIG)
    return pl.pallas_call(
        functools.partial(_attn_kernel, tq=tq, ck=ck, diff=False, lambda_init=0.0),
        grid=(batch, H_FOX, nq),
        in_specs=[
            pl.BlockSpec((None, tq, LANES), lambda b, h, i: (b, i, _QF + h)),
            pl.BlockSpec((None, seq, LANES), lambda b, h, i: (b, 0, _KF + h)),
            pl.BlockSpec((None, seq, LANES), lambda b, h, i: (b, 0, 0)),
            pl.BlockSpec((VT_ROWS, seq), lambda b, h, i: (h, b)),
        ],
        out_specs=pl.BlockSpec((None, tq, LANES), lambda b, h, i: (b, i, h)),
        out_shape=jax.ShapeDtypeStruct((batch, seq, H_FOX * D_FOX), BF16),
        scratch_shapes=_attn_scratch(tq, MXU_N, 2 * LANES),
        compiler_params=_cparams(("parallel", "parallel", "arbitrary")),
        name="fox_attn",
    )(big3, big3, kx, vt).reshape(batch * seq, H_FOX * D_FOX)


def _diff_attn(qk, vt, lam_vecs, subln_w, layer, lambda_init, *, batch, seq, tq=2048, ck=256):
    nq = seq // tq
    qk3 = qk.reshape(batch, seq, 2 * D_MODEL)
    return pl.pallas_call(
        functools.partial(_attn_kernel, tq=tq, ck=ck, diff=True, lambda_init=lambda_init),
        grid=(batch, H_DIFF, nq),
        in_specs=[
            pl.BlockSpec((None, tq, LANES), lambda b, h, i: (b, i, h)),
            pl.BlockSpec((None, seq, LANES), lambda b, h, i: (b, 0, H_DIFF + h)),
            pl.BlockSpec((VT_ROWS, seq), lambda b, h, i: (h, b)),
            pl.BlockSpec((None, 4, D_DIFF), lambda b, h, i: (layer, 0, 0)),
            pl.BlockSpec((None, 1, 2 * D_DIFF), lambda b, h, i: (layer, 0, 0)),
        ],
        out_specs=pl.BlockSpec((None, tq, LANES), lambda b, h, i: (b, i, h)),
        out_shape=jax.ShapeDtypeStruct((batch, seq, D_MODEL), BF16),
        scratch_shapes=_attn_scratch(tq, MXU_N // 2, LANES),
        compiler_params=_cparams(("parallel", "parallel", "arbitrary")),
        name="diff_attn",
    )(qk3, qk3, vt, lam_vecs, subln_w).reshape(batch * seq, D_MODEL)


def _od_proj_kernel(x_ref, nw_ref, w_ref, cos_ref, sin_ref, o_ref, vt_ref, hn_ref, *, tn):
    hn_ref[...] = _rms(x_ref[...], nw_ref[...]).astype(BF16)
    half = D_DIFF // 2
    lane = lax.broadcasted_iota(jnp.int32, (x_ref.shape[0], LANES), 1)
    first_half = (lane % D_DIFF) < half
    for c in range(2 * D_MODEL // tn):
        y = _dot(hn_ref[...], w_ref[:, c * tn:(c + 1) * tn])
        table = (c * tn) // D_MODEL
        cos = cos_ref[table]
        sin = sin_ref[table]
        for g in range(tn // LANES):
            yg = y[:, g * LANES:(g + 1) * LANES]
            rot = jnp.where(first_half, pltpu.roll(yg, LANES - half, 1), pltpu.roll(yg, half, 1))
            o_ref[:, c * tn + g * LANES:c * tn + (g + 1) * LANES] = (yg * cos + rot * sin).astype(BF16)
    for c in range(D_MODEL // tn):
        col0 = 2 * D_MODEL + c * tn
        _store_vt(vt_ref, _dot(hn_ref[...], w_ref[:, col0:col0 + tn]), c * (tn // LANES))


def _od_proj(x, nw, w, cos_t, sin_t, layer, *, seq, tm=512, tn=512):
    m = x.shape[0]
    sb = seq // tm
    return pl.pallas_call(
        functools.partial(_od_proj_kernel, tn=tn),
        grid=(m // tm,),
        in_specs=[
            pl.BlockSpec((tm, D_MODEL), lambda i: (i, 0)),
            _resident((None, 1, D_MODEL), lambda i: (layer, 0, 0)),
            _resident((None, D_MODEL, 3 * D_MODEL), lambda i: (layer, 0, 0)),
            pl.BlockSpec((2, tm, LANES), lambda i: (0, i % sb, 0)),
            pl.BlockSpec((2, tm, LANES), lambda i: (0, i % sb, 0)),
        ],
        out_specs=[
            pl.BlockSpec((tm, 2 * D_MODEL), lambda i: (i, 0)),
            pl.BlockSpec((H_DIFF * VT_ROWS, tm), lambda i: (0, i)),
        ],
        out_shape=[
            jax.ShapeDtypeStruct((m, 2 * D_MODEL), BF16),
            jax.ShapeDtypeStruct((H_DIFF * VT_ROWS, m), BF16),
        ],
        scratch_shapes=[pltpu.VMEM((tm, D_MODEL), BF16)],
        compiler_params=_cparams(("parallel",)),
        name="od_proj",
    )(x, nw, w, cos_t, sin_t)


def _rope_tables(seq):
    half = D_DIFF // 2
    inv = 1.0 / (ROPE_THETA ** (jnp.arange(half, dtype=F32) * 2.0 / D_DIFF))
    ang = jnp.arange(seq, dtype=F32)[:, None] * inv[None, :]
    cos = jnp.tile(jnp.cos(ang), (1, LANES // half))
    sin = jnp.tile(jnp.concatenate([-jnp.sin(ang), jnp.sin(ang)], axis=1), (1, LANES // D_DIFF))
    qs = (D_DIFF ** -0.5) * LOG2E
    return jnp.stack([cos * qs, cos]), jnp.stack([sin * qs, sin])


def _ffn_kernel(*refs, n_in, tm, tn, rows_per_seq, final_norm):
    x_ref, xp_ref = refs[:2]
    a_refs = refs[2:2 + n_in]
    ap_refs = refs[2 + n_in:2 + 2 * n_in]
    wo_ref, nw_ref, wup_ref, cw_ref, cb_ref, wd_ref, fw_ref, o_ref, hn_ref, hmid_ref = refs[2 + 2 * n_in:]
    halo = BF16_SUBLANES
    kw = a_refs[0].shape[1]

    xm = x_ref[...]
    xpm = xp_ref[...]
    for r in range(n_in):
        wo = wo_ref[r * kw:(r + 1) * kw, :]
        xm = xm + _dot(a_refs[r][...], wo)
        xpm = xpm + _dot(ap_refs[r][...], wo)
    seq_start = (pl.program_id(0) % rows_per_seq) == 0
    hn_ref[:halo, :] = jnp.where(seq_start, 0.0, _rms(xpm, nw_ref[...])).astype(BF16)
    hn_ref[halo:, :] = _rms(xm, nw_ref[...]).astype(BF16)

    for c in range(D_FF // tn):
        cols = slice(c * tn, (c + 1) * tn)
        ge = _dot(hn_ref[...], wup_ref[:, cols])
        gate = cb_ref[:, cols] + cw_ref[FFN_CONV - 1:FFN_CONV, cols] * ge[halo:]
        for t in range(FFN_CONV - 1):
            off = halo - (FFN_CONV - 1) + t
            gate = gate + cw_ref[t:t + 1, cols] * ge[off:off + tm]
        val = _dot(hn_ref[halo:, :], wup_ref[:, D_FF + c * tn:D_FF + (c + 1) * tn])
        hmid_ref[:, cols] = (_silu(gate) * val).astype(BF16)

    y = xm + _dot(hmid_ref[...], wd_ref[...])
    if final_norm:
        y = _rms(y, fw_ref[...])
    o_ref[...] = y


def _ffn(x, acts, w_out, out_layer, nw, w_up, conv_w, conv_b, w_down, final_w, layer, *, seq,
         final_norm, tm=512, tn=256):
    m = x.shape[0]
    n_in = len(acts)
    kw = acts[0].shape[1]
    hb = tm // BF16_SUBLANES
    prev = lambda i: (jnp.maximum(i * hb - 1, 0), 0)
    in_specs = [pl.BlockSpec((tm, D_MODEL), lambda i: (i, 0)),
                pl.BlockSpec((BF16_SUBLANES, D_MODEL), prev)]
    in_specs += [pl.BlockSpec((tm, kw), lambda i: (i, 0)) for _ in acts]
    in_specs += [pl.BlockSpec((BF16_SUBLANES, kw), prev) for _ in acts]
    in_specs += [
        _resident((None, D_MODEL, D_MODEL), lambda i: (out_layer, 0, 0)),
        _resident((None, 1, D_MODEL), lambda i: (layer, 0, 0)),
        _resident((None, D_MODEL, 2 * D_FF), lambda i: (layer, 0, 0)),
        _resident((None, FFN_CONV, D_FF), lambda i: (layer, 0, 0)),
        _resident((None, 1, D_FF), lambda i: (layer, 0, 0)),
        _resident((None, D_FF, D_MODEL), lambda i: (layer, 0, 0)),
        _resident((1, D_MODEL), lambda i: (0, 0)),
    ]
    return pl.pallas_call(
        functools.partial(_ffn_kernel, n_in=n_in, tm=tm, tn=tn, rows_per_seq=seq // tm,
                          final_norm=final_norm),
        grid=(m // tm,),
        in_specs=in_specs,
        out_specs=pl.BlockSpec((tm, D_MODEL), lambda i: (i, 0)),
        out_shape=jax.ShapeDtypeStruct((m, D_MODEL), F32),
        scratch_shapes=[pltpu.VMEM((tm + BF16_SUBLANES, D_MODEL), BF16),
                        pltpu.VMEM((tm, D_FF), BF16)],
        compiler_params=_cparams(("parallel",)),
        name="conv_ffn",
    )(x, x, *acts, *acts, w_out, nw, w_up, conv_w, conv_b, w_down, final_w)


def _gate_rows(vals, start):
    z = jnp.zeros((vals.shape[0], GATE_ROWS, LANES), F32)
    return z.at[:, start:start + vals.shape[1], :].set(vals[:, :, None])


def kernel(x, ev_norm_w, ev_w_in, dn_conv_w, dn_a_log, dn_dt_bias, dn_norm_w, fox_f_bias, ev_w_out,
           od_norm_w, od_w_in, diff_lambda, diff_subln_w, od_w_out,
           ffn_norm_w, ffn_w_up, ffn_conv_w, ffn_conv_b, ffn_w_down, final_norm_w):
    batch, seq, _ = x.shape
    depth = ffn_norm_w.shape[0]
    h = x.reshape(batch * seq, D_MODEL)

    fox_v1 = EV_FOX0 + 3 * H_FOX * D_FOX
    ev_w_a = ev_w_in[:, :, :EV_GATE0].astype(BF16)
    ev_w_fox = ev_w_in[:, :, EV_FOX0:fox_v1].astype(BF16)
    ev_w_small = jnp.concatenate(
        [ev_w_in[:, :, EV_GATE0:EV_FOX0], ev_w_in[:, :, fox_v1:],
         jnp.zeros((ev_w_in.shape[0], D_MODEL, LANES - 2 * H_DN - H_FOX), F32)], axis=2).astype(BF16)
    ev_w_out_b = ev_w_out.astype(BF16)
    od_w_in_b = od_w_in.astype(BF16)
    od_w_out_b = od_w_out.astype(BF16)
    ffn_w_up_b = ffn_w_up.astype(BF16)
    ffn_w_down_b = ffn_w_down.astype(BF16)
    nega = _gate_rows(-jnp.exp(dn_a_log), _G_DECAY)
    dtb = _gate_rows(dn_dt_bias, _G_DECAY)
    fb = _gate_rows(fox_f_bias, _G_FOX)
    cos_t, sin_t = _rope_tables(seq)
    row3 = lambda a: a[:, None, :]

    for i in range(depth):
        j = i // 2
        if i % 2 == 0:
            big, vt, small = _ev_proj(h, row3(ev_norm_w), ev_w_a, ev_w_fox, ev_w_small, j)
            gcol, grow, kx = _gates(small, nega[j], dtb[j], fb[j], batch=batch, seq=seq)
            grow4 = grow.reshape(batch, GATE_ROWS, seq // DN_CHUNK, DN_CHUNK).transpose(0, 2, 1, 3)
            o_d = _deltanet(big, dn_conv_w, gcol, grow4, row3(dn_norm_w), j, batch=batch, seq=seq)
            o_f = _fox(big, kx, vt, batch=batch, seq=seq)
            acts, w_out = [o_d, o_f], ev_w_out_b
        else:
            lambda_init = 0.8 - 0.6 * math.exp(-0.3 * i)
            qk, vt = _od_proj(h, row3(od_norm_w), od_w_in_b, cos_t, sin_t, j, seq=seq)
            o = _diff_attn(qk, vt, diff_lambda, row3(diff_subln_w), j, lambda_init, batch=batch, seq=seq)
            acts, w_out = [o], od_w_out_b
        h = _ffn(h, acts, w_out, j, row3(ffn_norm_w), ffn_w_up_b, ffn_conv_w, row3(ffn_conv_b),
                 ffn_w_down_b, final_norm_w[None, :], i, seq=seq, final_norm=(i == depth - 1))
    return h.reshape(batch, seq, D_MODEL)
```

```python
import functools
import math

import jax
import jax.numpy as jnp
from jax import lax
from jax.experimental import pallas as pl
from jax.experimental.pallas import tpu as pltpu

F32 = jnp.float32
BF16 = jnp.bfloat16

D_MODEL = 1024
H_DN = 4
DK_DN = 128
DV_DN = 128
DN_CONV = 4
DN_CHUNK = 64
H_FOX = 4
D_FOX = 128
D_DIFF = 64
H_DIFF = D_MODEL // (2 * D_DIFF)
ROPE_THETA = 10000.0
D_FF = 128 * ((8 * D_MODEL // 3 + 127) // 128)
FFN_CONV = 3
EPS = 1e-6
LOG2E = 1.4426950408889634

LANES = 128
MXU_N = 256
BF16_SUBLANES = 16
VMEM_LIMIT = 56 * 1024 * 1024
VT_ROWS = LANES + BF16_SUBLANES
CB_TERMS = 3
KX_STRIDE = 4
GATE_ROWS = 16

DN_QKV = H_DN * (2 * DK_DN + DV_DN)
EV_GATE0 = DN_QKV + H_DN * DV_DN
EV_FOX0 = EV_GATE0 + 2 * H_DN
EV_BIG = EV_GATE0 + 2 * H_FOX * D_FOX
_QF, _KF = EV_GATE0 // LANES, EV_GATE0 // LANES + H_FOX
_G_BETA, _G_DECAY, _G_FOX = 0, 4, 8


def _cparams(sem):
    return pltpu.CompilerParams(dimension_semantics=sem, vmem_limit_bytes=VMEM_LIMIT)


def _resident(block_shape, index_map):
    return pl.BlockSpec(block_shape, index_map, pipeline_mode=pl.Buffered(1))


def _rms(x, w):
    return x * lax.rsqrt(jnp.mean(x * x, axis=-1, keepdims=True) + EPS) * w


def _sigmoid(x):
    return 1.0 / (1.0 + jnp.exp(-x))


def _silu(x):
    return x * _sigmoid(x)


def _dot(a, b):
    return jnp.dot(a, b, preferred_element_type=F32)


def _dot_nt(a, b):
    return lax.dot_general(a, b, (((1,), (1,)), ((), ())), preferred_element_type=F32)


def _store_vt(vt_ref, v, head0):
    vt = v.astype(BF16).T
    ones = jnp.ones((VT_ROWS - LANES, vt.shape[1]), BF16)
    for h in range(vt.shape[0] // LANES):
        row0 = (head0 + h) * VT_ROWS
        vt_ref[row0:row0 + LANES, :] = vt[h * LANES:(h + 1) * LANES]
        vt_ref[row0 + LANES:row0 + VT_ROWS, :] = ones


def _ev_proj_kernel(x_ref, nw_ref, wa_ref, wf_ref, ws_ref, big_ref, vt_ref, small_ref, hn_ref, *, tn):
    hn_ref[...] = _rms(x_ref[...], nw_ref[...]).astype(BF16)
    small_ref[...] = _dot(hn_ref[...], ws_ref[...])
    for c in range(EV_GATE0 // tn):
        cols = slice(c * tn, (c + 1) * tn)
        big_ref[:, cols] = _dot(hn_ref[...], wa_ref[:, cols]).astype(BF16)
    n_qk = 2 * H_FOX * D_FOX
    for c in range(n_qk // tn):
        y = _dot(hn_ref[...], wf_ref[:, c * tn:(c + 1) * tn])
        if (c + 1) * tn <= H_FOX * D_FOX:
            y = y * ((D_FOX ** -0.5) * LOG2E)
        big_ref[:, EV_GATE0 + c * tn:EV_GATE0 + (c + 1) * tn] = y.astype(BF16)
    _store_vt(vt_ref, _dot(hn_ref[...], wf_ref[:, n_qk:]), 0)


def _ev_proj(x, nw, w_a, w_fox, w_small, layer, *, tm=1024, tn=512):
    m = x.shape[0]
    return pl.pallas_call(
        functools.partial(_ev_proj_kernel, tn=tn),
        grid=(m // tm,),
        in_specs=[
            pl.BlockSpec((tm, D_MODEL), lambda i: (i, 0)),
            _resident((None, 1, D_MODEL), lambda i: (layer, 0, 0)),
            _resident((None, D_MODEL, EV_GATE0), lambda i: (layer, 0, 0)),
            _resident((None, D_MODEL, w_fox.shape[2]), lambda i: (layer, 0, 0)),
            _resident((None, D_MODEL, LANES), lambda i: (layer, 0, 0)),
        ],
        out_specs=[
            pl.BlockSpec((tm, EV_BIG), lambda i: (i, 0)),
            pl.BlockSpec((H_FOX * VT_ROWS, tm), lambda i: (0, i)),
            pl.BlockSpec((tm, LANES), lambda i: (i, 0)),
        ],
        out_shape=[
            jax.ShapeDtypeStruct((m, EV_BIG), BF16),
            jax.ShapeDtypeStruct((H_FOX * VT_ROWS, m), BF16),
            jax.ShapeDtypeStruct((m, LANES), F32),
        ],
        scratch_shapes=[pltpu.VMEM((tm, D_MODEL), BF16)],
        compiler_params=_cparams(("parallel",)),
        name="ev_proj",
    )(x, nw, w_a, w_fox, w_small)


def _gate_kernel(lg_ref, nega_ref, dtb_ref, fb_ref, col_ref, row_ref, kx_ref, *, seq):
    t = lg_ref[...].T[:GATE_ROWS, :]
    ch = lax.broadcasted_iota(jnp.int32, t.shape, 0)
    pos = lax.broadcasted_iota(jnp.int32, t.shape, 1)
    is_decay = (ch >= _G_DECAY) & (ch < _G_FOX)
    is_fox = (ch >= _G_FOX) & (ch < _G_FOX + H_FOX)
    beta = _sigmoid(t)
    td = t + dtb_ref[:, :1]
    softplus = jnp.maximum(td, 0.0) + jnp.log(1.0 + jnp.exp(-jnp.abs(td)))
    g = nega_ref[:, :1] * softplus
    tf = t + fb_ref[:, :1]
    logf = (jnp.minimum(tf, 0.0) - jnp.log(1.0 + jnp.exp(-jnp.abs(tf)))) * LOG2E
    v = jnp.where(is_decay, g, jnp.where(is_fox, logf, 0.0))
    idx = jnp.where(is_decay, pos % DN_CHUNK, jnp.where(is_fox, pos, 0))
    k = 1
    while k < seq:
        shifted = pltpu.roll(v, k, 1)
        v = v + jnp.where(idx >= k, shifted, 0.0)
        k *= 2
    out = jnp.where(ch < _G_DECAY, beta, v)
    row_ref[...] = out
    pad = jnp.zeros((LANES - GATE_ROWS, seq), F32)
    col_ref[...] = jnp.concatenate([out, pad], axis=0).T
    ext = jnp.zeros(t.shape, F32)
    for h in range(H_FOX):
        rest = v[_G_FOX + h:_G_FOX + h + 1, :]
        for term in range(CB_TERMS):
            part = rest.astype(BF16).astype(F32)
            ext = jnp.where(ch == KX_STRIDE * h + term, part, ext)
            rest = rest - part
    kx_ref[...] = jnp.concatenate([ext, pad], axis=0).T.astype(BF16)


def _gates(small, nega, dtb, fb, *, batch, seq):
    return pl.pallas_call(
        functools.partial(_gate_kernel, seq=seq),
        grid=(batch,),
        in_specs=[
            pl.BlockSpec((seq, LANES), lambda b: (b, 0)),
            pl.BlockSpec((GATE_ROWS, LANES), lambda b: (0, 0)),
            pl.BlockSpec((GATE_ROWS, LANES), lambda b: (0, 0)),
            pl.BlockSpec((GATE_ROWS, LANES), lambda b: (0, 0)),
        ],
        out_specs=[
            pl.BlockSpec((seq, LANES), lambda b: (b, 0)),
            pl.BlockSpec((None, GATE_ROWS, seq), lambda b: (b, 0, 0)),
            pl.BlockSpec((None, seq, LANES), lambda b: (b, 0, 0)),
        ],
        out_shape=[
            jax.ShapeDtypeStruct((batch * seq, LANES), F32),
            jax.ShapeDtypeStruct((batch, GATE_ROWS, seq), F32),
            jax.ShapeDtypeStruct((batch, seq, LANES), BF16),
        ],
        compiler_params=_cparams(("parallel",)),
        name="gates",
    )(small, nega, dtb, fb)


def _dn_kernel(q_ref, k_ref, v_ref, qh_ref, kh_ref, vh_ref, z_ref, cq_ref, ck_ref, cv_ref,
               gcol_ref, grow_ref, nw_ref, o_ref, state_ref, *, ts):
    si = pl.program_id(1)
    c = DN_CHUNK
    nc = ts // c
    halo = BF16_SUBLANES
    heads = range(H_DN)

    @pl.when(si == 0)
    def _():
        state_ref[...] = jnp.zeros_like(state_ref)

    def conv_silu(x_ref, halo_ref, cw_ref):
        x = x_ref[...].astype(F32)
        prev = jnp.where(si == 0, 0.0, halo_ref[...].astype(F32))
        xe = jnp.concatenate([prev, x], axis=0)
        cw = cw_ref[...]
        y = cw[DN_CONV - 1:DN_CONV] * x
        for j in range(DN_CONV - 1):
            off = halo - (DN_CONV - 1) + j
            y = y + cw[j:j + 1] * xe[off:off + ts]
        return _silu(y)

    def l2n(x):
        return x * lax.rsqrt(jnp.sum(x * x, axis=-1, keepdims=True) + EPS)

    qa = conv_silu(q_ref, qh_ref, cq_ref)
    ka = conv_silu(k_ref, kh_ref, ck_ref)
    va = conv_silu(v_ref, vh_ref, cv_ref)
    gates = gcol_ref[...]
    lane = lax.broadcasted_iota(jnp.int32, gates.shape, 1)

    r2 = lax.broadcasted_iota(jnp.int32, (c, c), 0)
    c2 = lax.broadcasted_iota(jnp.int32, (c, c), 1)
    incl = r2 >= c2
    incl_f = incl.astype(F32)
    strict_f = (r2 > c2).astype(F32)

    def level_mask(s):
        hit = ((r2 // (2 * s)) == (c2 // (2 * s))) & ((r2 // s) % 2 == 1) & ((c2 // s) % 2 == 0)
        return hit.astype(F32)

    bdot_nt = lambda a, b: jnp.einsum('cid,cjd->cij', a, b, preferred_element_type=F32)
    bdot = lambda a, b: jnp.einsum('cij,cjd->cid', a, b, preferred_element_type=F32)
    chunks = lambda a: a.reshape(nc, c, a.shape[-1])

    k3, kb3, q3, rhs, qg3, kd3, egl, decay = [], [], [], [], [], [], [], []
    for h in heads:
        cols = slice(h * DK_DN, (h + 1) * DK_DN)
        q = l2n(qa[:, cols]) * (DK_DN ** -0.5)
        k = l2n(ka[:, cols])
        v = va[:, cols]
        beta = jnp.sum(jnp.where(lane == _G_BETA + h, gates, 0.0), axis=1, keepdims=True)
        gc = jnp.sum(jnp.where(lane == _G_DECAY + h, gates, 0.0), axis=1, keepdims=True)
        g_row = grow_ref[:, _G_DECAY + h:_G_DECAY + h + 1, :]
        gc3 = chunks(gc)
        gl3 = gc3[:, c - 1:c, :]
        eg = jnp.exp(gc)
        kb = k * beta
        k3.append(chunks(k).astype(BF16))
        kb3.append(chunks(kb).astype(BF16))
        q3.append(chunks(q).astype(BF16))
        rhs.append(jnp.concatenate([chunks(v * beta), chunks(kb * eg)], axis=2))
        qg3.append(chunks(q * eg))
        kd3.append((chunks(k) * jnp.exp(gl3 - gc3)).astype(BF16))
        egl.append(jnp.exp(gl3))
        decay.append(jnp.exp(jnp.where(incl[None], gc3 - g_row, 0.0)))

    lower = [bdot_nt(kb3[h], k3[h]) * (decay[h] * strict_f[None]) for h in heads]
    qk = [(bdot_nt(q3[h], k3[h]) * (decay[h] * incl_f[None])).astype(BF16) for h in heads]

    first = level_mask(1)
    xs = [-(lower[h] * first[None]) for h in heads]
    s = 2
    while s < c:
        lm = level_mask(s)
        cs = [lower[h] * lm[None] for h in heads]
        y = [cs[h] + bdot(cs[h].astype(BF16), xs[h].astype(BF16)) for h in heads]
        xs = [xs[h] - y[h] - bdot(xs[h].astype(BF16), y[h].astype(BF16)) for h in heads]
        s *= 2

    uwb = [(rhs[h] + bdot(xs[h].astype(BF16), rhs[h].astype(BF16))).astype(BF16) for h in heads]

    quw = [bdot(qk[h], uwb[h]) for h in heads]
    o_in = [quw[h][:, :, :DV_DN] for h in heads]
    q_st = [(qg3[h] - quw[h][:, :, DV_DN:]).astype(BF16) for h in heads]
    kuw = [[lax.dot_general(kd3[h][n], uwb[h][n], (((0,), (0,)), ((), ())),
                            preferred_element_type=F32) for n in range(nc)]
           for h in heads]

    state = [state_ref[h] for h in heads]
    outs = [[] for _ in heads]
    for n in range(nc):
        lhs = [jnp.concatenate([kuw[h][n][:, DV_DN:].astype(BF16), q_st[h][n]], axis=0)
               for h in heads]
        r = [_dot(lhs[h], state[h].astype(BF16)) for h in heads]
        for h in heads:
            outs[h].append(r[h][DK_DN:] + o_in[h][n])
        state = [state[h] * egl[h][n] + kuw[h][n][:, :DV_DN] - r[h][:DK_DN] for h in heads]
    for h in heads:
        cols = slice(h * DV_DN, (h + 1) * DV_DN)
        state_ref[h] = state[h]
        o = jnp.concatenate(outs[h], axis=0)
        o = _rms(o, nw_ref[...]) * _silu(z_ref[:, cols].astype(F32))
        o_ref[:, cols] = o.astype(BF16)


def _deltanet(big, conv_w, gcol, grow4, norm_w, layer, *, batch, seq, ts=512):
    m = batch * seq
    ns = seq // ts
    nc = ts // DN_CHUNK
    hb = ts // BF16_SUBLANES
    w = H_DN * DK_DN

    def blk(col):
        return pl.BlockSpec((ts, w), lambda b, s: (b * ns + s, col))

    def halo(col):
        return pl.BlockSpec((BF16_SUBLANES, w),
                            lambda b, s: (jnp.maximum((b * ns + s) * hb - 1, 0), col))

    def cw(col):
        return pl.BlockSpec((None, DN_CONV, w), lambda b, s: (layer, 0, col))

    return pl.pallas_call(
        functools.partial(_dn_kernel, ts=ts),
        grid=(batch, ns),
        in_specs=[
            blk(0), blk(1), blk(2), halo(0), halo(1), halo(2), blk(3),
            cw(0), cw(1), cw(2),
            pl.BlockSpec((ts, LANES), lambda b, s: (b * ns + s, 0)),
            pl.BlockSpec((None, nc, 16, DN_CHUNK), lambda b, s: (b, s, 0, 0)),
            pl.BlockSpec((None, 1, DV_DN), lambda b, s: (layer, 0, 0)),
        ],
        out_specs=pl.BlockSpec((ts, w), lambda b, s: (b * ns + s, 0)),
        out_shape=jax.ShapeDtypeStruct((m, H_DN * DV_DN), BF16),
        scratch_shapes=[pltpu.VMEM((H_DN, DK_DN, DV_DN), F32)],
        compiler_params=_cparams(("parallel", "arbitrary")),
        name="deltanet",
    )(big, big, big, big, big, big, big, conv_w, conv_w, conv_w, gcol, grow4, norm_w)


def _attn_kernel(*refs, tq, ck, diff, lambda_init):
    if diff:
        q_ref, k_ref, vt_ref, lv_ref, sw_ref, o_ref, qz_ref, m_ref, acc_ref = refs
        rsub = MXU_N // 2
    else:
        q_ref, k_ref, kx_ref, vt_ref, o_ref, qz_ref, m_ref, acc_ref = refs
        rsub = MXU_N
    nsub = tq // rsub
    ahead = 5
    kpi = 4 if diff else 8
    qi = pl.program_id(2)

    for r in range(nsub):
        rows = q_ref[r * rsub:(r + 1) * rsub, :]
        lane = lax.broadcasted_iota(jnp.int32, rows.shape, 1)
        if diff:
            qz_ref[r, :rsub, :] = jnp.where(lane < D_DIFF, rows, jnp.zeros_like(rows))
            qz_ref[r, rsub:, :] = jnp.where(lane >= D_DIFF, rows, jnp.zeros_like(rows))
        else:
            qz_ref[r, :, :LANES] = rows
            lane0 = KX_STRIDE * pl.program_id(1)
            own = (lane >= lane0) & (lane < lane0 + CB_TERMS)
            qz_ref[r, :, LANES:] = jnp.where(own, -1.0, 0.0).astype(BF16)
    m_ref[...] = jnp.full_like(m_ref, -jnp.inf)
    acc_ref[...] = jnp.zeros_like(acc_ref)

    def run(tiles):
        loaded = {}

        def operands(t):
            if t not in loaded:
                ks = tiles[t][0]
                k = k_ref[pl.ds(ks, ck), :]
                if not diff:
                    k = jnp.concatenate([k, kx_ref[pl.ds(ks, ck), :]], axis=1)
                loaded[t] = (k, vt_ref[:, pl.ds(ks, ck)])
            return loaded[t]

        items = [(t, r) for t, (_, plan) in enumerate(tiles) for r in range(nsub)
                 if plan[r] != "skip"]
        scores = {}

        def issue(item):
            scores[item] = _dot_nt(operands(item[0])[0], qz_ref[item[1]])

        for item in items[:ahead]:
            issue(item)
        for idx, (t, r) in enumerate(items):
            if idx + ahead < len(items):
                issue(items[idx + ahead])
            s = scores.pop((t, r))
            vt = operands(t)[1]
            plan = tiles[t][1]
            if plan[r] is not None:
                kpos = lax.broadcasted_iota(jnp.int32, s.shape, 0) + plan[r]
                qpos = lax.broadcasted_iota(jnp.int32, s.shape, 1)
                if diff:
                    qpos = qpos % rsub
                s = jnp.where(kpos <= qpos + r * rsub, s, -jnp.inf)
            m_prev = m_ref[r]
            m_new = jnp.maximum(m_prev, jnp.max(s, axis=0, keepdims=True))
            p = jnp.exp2(s - m_new)
            acc_ref[r] = jnp.exp2(m_prev - m_new) * acc_ref[r] + _dot(vt, p.astype(BF16))
            m_ref[r] = m_new

    def body(t, carry):
        base = t * (kpi * ck)
        run([(pl.multiple_of(base + u * ck, ck), [None] * nsub) for u in range(kpi)])
        return carry

    lax.fori_loop(0, qi * (tq // (kpi * ck)), body, 0)
    diag = []
    for jd in range(tq // ck):
        plan = []
        for r in range(nsub):
            if jd * ck > r * rsub + rsub - 1:
                plan.append("skip")
            elif jd * ck + ck - 1 <= r * rsub:
                plan.append(None)
            else:
                plan.append(jd * ck)
        diag.append((pl.multiple_of(qi * tq + jd * ck, ck), plan))
    run(diag)

    if diff:
        lv = lv_ref[...]
        lam = (jnp.exp(jnp.sum(lv[0:1] * lv[1:2], axis=1, keepdims=True))
               - jnp.exp(jnp.sum(lv[2:3] * lv[3:4], axis=1, keepdims=True)) + lambda_init)
    for r in range(nsub):
        tot = acc_ref[r]
        ot = tot[:LANES] / tot[LANES:LANES + 1]
        if diff:
            o = (ot[:, :rsub] - lam * ot[:, rsub:]).T
            o = _rms(o, sw_ref[...]) * (1.0 - lambda_init)
        else:
            o = ot.T
        o_ref[r * rsub:(r + 1) * rsub, :] = o.astype(BF16)


def _attn_scratch(tq, rsub, kdim):
    nsub = tq // rsub
    return [pltpu.VMEM((nsub, MXU_N, kdim), BF16), pltpu.VMEM((nsub, 1, MXU_N), F32),
            pltpu.VMEM((nsub, VT_ROWS, MXU_N), F32)]


def _fox(big, kx, vt, *, batch, seq, tq=2048, ck=256):
    nq = seq // tq
    big3 = big.reshape(batch, seq, EV_BIG)
    return pl.pallas_call(
        functools.partial(_attn_kernel, tq=tq, ck=ck, diff=False, lambda_init=0.0),
        grid=(batch, H_FOX, nq),
        in_specs=[
            pl.BlockSpec((None, tq, LANES), lambda b, h, i: (b, i, _QF + h)),
            pl.BlockSpec((None, seq, LANES), lambda b, h, i: (b, 0, _KF + h)),
            pl.BlockSpec((None, seq, LANES), lambda b, h, i: (b, 0, 0)),
            pl.BlockSpec((VT_ROWS, seq), lambda b, h, i: (h, b)),
        ],
        out_specs=pl.BlockSpec((None, tq, LANES), lambda b, h, i: (b, i, h)),
        out_shape=jax.ShapeDtypeStruct((batch, seq, H_FOX * D_FOX), BF16),
        scratch_shapes=_attn_scratch(tq, MXU_N, 2 * LANES),
        compiler_params=_cparams(("parallel", "parallel", "arbitrary")),
        name="fox_attn",
    )(big3, big3, kx, vt).reshape(batch * seq, H_FOX * D_FOX)


def _diff_attn(qk, vt, lam_vecs, subln_w, layer, lambda_init, *, batch, seq, tq=2048, ck=256):
    nq = seq // tq
    qk3 = qk.reshape(batch, seq, 2 * D_MODEL)
    return pl.pallas_call(
        functools.partial(_attn_kernel, tq=tq, ck=ck, diff=True, lambda_init=lambda_init),
        grid=(batch, H_DIFF, nq),
        in_specs=[
            pl.BlockSpec((None, tq, LANES), lambda b, h, i: (b, i, h)),
            pl.BlockSpec((None, seq, LANES), lambda b, h, i: (b, 0, H_DIFF + h)),
            pl.BlockSpec((VT_ROWS, seq), lambda b, h, i: (h, b)),
            pl.BlockSpec((None, 4, D_DIFF), lambda b, h, i: (layer, 0, 0)),
            pl.BlockSpec((None, 1, 2 * D_DIFF), lambda b, h, i: (layer, 0, 0)),
        ],
        out_specs=pl.BlockSpec((None, tq, LANES), lambda b, h, i: (b, i, h)),
        out_shape=jax.ShapeDtypeStruct((batch, seq, D_MODEL), BF16),
        scratch_shapes=_attn_scratch(tq, MXU_N // 2, LANES),
        compiler_params=_cparams(("parallel", "parallel", "arbitrary")),
        name="diff_attn",
    )(qk3, qk3, vt, lam_vecs, subln_w).reshape(batch * seq, D_MODEL)


def _od_proj_kernel(x_ref, nw_ref, w_ref, cos_ref, sin_ref, o_ref, vt_ref, hn_ref, *, tn):
    hn_ref[...] = _rms(x_ref[...], nw_ref[...]).astype(BF16)
    half = D_DIFF // 2
    lane = lax.broadcasted_iota(jnp.int32, (x_ref.shape[0], LANES), 1)
    first_half = (lane % D_DIFF) < half
    for c in range(2 * D_MODEL // tn):
        y = _dot(hn_ref[...], w_ref[:, c * tn:(c + 1) * tn])
        table = (c * tn) // D_MODEL
        cos = cos_ref[table]
        sin = sin_ref[table]
        for g in range(tn // LANES):
            yg = y[:, g * LANES:(g + 1) * LANES]
            rot = jnp.where(first_half, pltpu.roll(yg, LANES - half, 1), pltpu.roll(yg, half, 1))
            o_ref[:, c * tn + g * LANES:c * tn + (g + 1) * LANES] = (yg * cos + rot * sin).astype(BF16)
    for c in range(D_MODEL // tn):
        col0 = 2 * D_MODEL + c * tn
        _store_vt(vt_ref, _dot(hn_ref[...], w_ref[:, col0:col0 + tn]), c * (tn // LANES))


def _od_proj(x, nw, w, cos_t, sin_t, layer, *, seq, tm=1024, tn=512):
    m = x.shape[0]
    sb = seq // tm
    return pl.pallas_call(
        functools.partial(_od_proj_kernel, tn=tn),
        grid=(m // tm,),
        in_specs=[
            pl.BlockSpec((tm, D_MODEL), lambda i: (i, 0)),
            _resident((None, 1, D_MODEL), lambda i: (layer, 0, 0)),
            _resident((None, D_MODEL, 3 * D_MODEL), lambda i: (layer, 0, 0)),
            pl.BlockSpec((2, tm, LANES), lambda i: (0, i % sb, 0)),
            pl.BlockSpec((2, tm, LANES), lambda i: (0, i % sb, 0)),
        ],
        out_specs=[
            pl.BlockSpec((tm, 2 * D_MODEL), lambda i: (i, 0)),
            pl.BlockSpec((H_DIFF * VT_ROWS, tm), lambda i: (0, i)),
        ],
        out_shape=[
            jax.ShapeDtypeStruct((m, 2 * D_MODEL), BF16),
            jax.ShapeDtypeStruct((H_DIFF * VT_ROWS, m), BF16),
        ],
        scratch_shapes=[pltpu.VMEM((tm, D_MODEL), BF16)],
        compiler_params=_cparams(("parallel",)),
        name="od_proj",
    )(x, nw, w, cos_t, sin_t)


def _rope_tables(seq):
    half = D_DIFF // 2
    inv = 1.0 / (ROPE_THETA ** (jnp.arange(half, dtype=F32) * 2.0 / D_DIFF))
    ang = jnp.arange(seq, dtype=F32)[:, None] * inv[None, :]
    cos = jnp.tile(jnp.cos(ang), (1, LANES // half))
    sin = jnp.tile(jnp.concatenate([-jnp.sin(ang), jnp.sin(ang)], axis=1), (1, LANES // D_DIFF))
    qs = (D_DIFF ** -0.5) * LOG2E
    return jnp.stack([cos * qs, cos]), jnp.stack([sin * qs, sin])


def _ffn_kernel(*refs, n_in, tm, tn, rows_per_seq, final_norm):
    x_ref, xp_ref = refs[:2]
    a_refs = refs[2:2 + n_in]
    ap_refs = refs[2 + n_in:2 + 2 * n_in]
    wo_ref, nw_ref, wup_ref, cw_ref, cb_ref, wd_ref, fw_ref, o_ref, hn_ref, hmid_ref = refs[2 + 2 * n_in:]
    halo = BF16_SUBLANES
    kw = a_refs[0].shape[1]

    xm = x_ref[...]
    xpm = xp_ref[...]
    for r in range(n_in):
        wo = wo_ref[r * kw:(r + 1) * kw, :]
        xm = xm + _dot(a_refs[r][...], wo)
        xpm = xpm + _dot(ap_refs[r][...], wo)
    seq_start = (pl.program_id(0) % rows_per_seq) == 0
    hn_ref[:halo, :] = jnp.where(seq_start, 0.0, _rms(xpm, nw_ref[...])).astype(BF16)
    hn_ref[halo:, :] = _rms(xm, nw_ref[...]).astype(BF16)

    for c in range(D_FF // tn):
        cols = slice(c * tn, (c + 1) * tn)
        ge = _dot(hn_ref[...], wup_ref[:, cols])
        gate = cb_ref[:, cols] + cw_ref[FFN_CONV - 1:FFN_CONV, cols] * ge[halo:]
        for t in range(FFN_CONV - 1):
            off = halo - (FFN_CONV - 1) + t
            gate = gate + cw_ref[t:t + 1, cols] * ge[off:off + tm]
        val = _dot(hn_ref[halo:, :], wup_ref[:, D_FF + c * tn:D_FF + (c + 1) * tn])
        hmid_ref[:, cols] = (_silu(gate) * val).astype(BF16)

    y = xm + _dot(hmid_ref[...], wd_ref[...])
    if final_norm:
        y = _rms(y, fw_ref[...])
    o_ref[...] = y


def _ffn(x, acts, w_out, out_layer, nw, w_up, conv_w, conv_b, w_down, final_w, layer, *, seq,
         final_norm, tm=1024, tn=256):
    m = x.shape[0]
    n_in = len(acts)
    kw = acts[0].shape[1]
    hb = tm // BF16_SUBLANES
    prev = lambda i: (jnp.maximum(i * hb - 1, 0), 0)
    in_specs = [pl.BlockSpec((tm, D_MODEL), lambda i: (i, 0)),
                pl.BlockSpec((BF16_SUBLANES, D_MODEL), prev)]
    in_specs += [pl.BlockSpec((tm, kw), lambda i: (i, 0)) for _ in acts]
    in_specs += [pl.BlockSpec((BF16_SUBLANES, kw), prev) for _ in acts]
    in_specs += [
        _resident((None, D_MODEL, D_MODEL), lambda i: (out_layer, 0, 0)),
        _resident((None, 1, D_MODEL), lambda i: (layer, 0, 0)),
        _resident((None, D_MODEL, 2 * D_FF), lambda i: (layer, 0, 0)),
        _resident((None, FFN_CONV, D_FF), lambda i: (layer, 0, 0)),
        _resident((None, 1, D_FF), lambda i: (layer, 0, 0)),
        _resident((None, D_FF, D_MODEL), lambda i: (layer, 0, 0)),
        _resident((1, D_MODEL), lambda i: (0, 0)),
    ]
    return pl.pallas_call(
        functools.partial(_ffn_kernel, n_in=n_in, tm=tm, tn=tn, rows_per_seq=seq // tm,
                          final_norm=final_norm),
        grid=(m // tm,),
        in_specs=in_specs,
        out_specs=pl.BlockSpec((tm, D_MODEL), lambda i: (i, 0)),
        out_shape=jax.ShapeDtypeStruct((m, D_MODEL), F32),
        scratch_shapes=[pltpu.VMEM((tm + BF16_SUBLANES, D_MODEL), BF16),
                        pltpu.VMEM((tm, D_FF), BF16)],
        compiler_params=_cparams(("parallel",)),
        name="conv_ffn",
    )(x, x, *acts, *acts, w_out, nw, w_up, conv_w, conv_b, w_down, final_w)


def _gate_rows(vals, start):
    z = jnp.zeros((vals.shape[0], GATE_ROWS, LANES), F32)
    return z.at[:, start:start + vals.shape[1], :].set(vals[:, :, None])


def kernel(x, ev_norm_w, ev_w_in, dn_conv_w, dn_a_log, dn_dt_bias, dn_norm_w, fox_f_bias, ev_w_out,
           od_norm_w, od_w_in, diff_lambda, diff_subln_w, od_w_out,
           ffn_norm_w, ffn_w_up, ffn_conv_w, ffn_conv_b, ffn_w_down, final_norm_w):
    batch, seq, _ = x.shape
    depth = ffn_norm_w.shape[0]
    h = x.reshape(batch * seq, D_MODEL)

    fox_v1 = EV_FOX0 + 3 * H_FOX * D_FOX
    ev_w_a = ev_w_in[:, :, :EV_GATE0].astype(BF16)
    ev_w_fox = ev_w_in[:, :, EV_FOX0:fox_v1].astype(BF16)
    ev_w_small = jnp.concatenate(
        [ev_w_in[:, :, EV_GATE0:EV_FOX0], ev_w_in[:, :, fox_v1:],
         jnp.zeros((ev_w_in.shape[0], D_MODEL, LANES - 2 * H_DN - H_FOX), F32)], axis=2).astype(BF16)
    ev_w_out_b = ev_w_out.astype(BF16)
    od_w_in_b = od_w_in.astype(BF16)
    od_w_out_b = od_w_out.astype(BF16)
    ffn_w_up_b = ffn_w_up.astype(BF16)
    ffn_w_down_b = ffn_w_down.astype(BF16)
    nega = _gate_rows(-jnp.exp(dn_a_log), _G_DECAY)
    dtb = _gate_rows(dn_dt_bias, _G_DECAY)
    fb = _gate_rows(fox_f_bias, _G_FOX)
    cos_t, sin_t = _rope_tables(seq)
    row3 = lambda a: a[:, None, :]

    for i in range(depth):
        j = i // 2
        if i % 2 == 0:
            big, vt, small = _ev_proj(h, row3(ev_norm_w), ev_w_a, ev_w_fox, ev_w_small, j)
            gcol, grow, kx = _gates(small, nega[j], dtb[j], fb[j], batch=batch, seq=seq)
            grow4 = grow.reshape(batch, GATE_ROWS, seq // DN_CHUNK, DN_CHUNK).transpose(0, 2, 1, 3)
            o_d = _deltanet(big, dn_conv_w, gcol, grow4, row3(dn_norm_w), j, batch=batch, seq=seq)
            o_f = _fox(big, kx, vt, batch=batch, seq=seq)
            acts, w_out = [o_d, o_f], ev_w_out_b
        else:
            lambda_init = 0.8 - 0.6 * math.exp(-0.3 * i)
            qk, vt = _od_proj(h, row3(od_norm_w), od_w_in_b, cos_t, sin_t, j, seq=seq)
            o = _diff_attn(qk, vt, diff_lambda, row3(diff_subln_w), j, lambda_init, batch=batch, seq=seq)
            acts, w_out = [o], od_w_out_b
        h = _ffn(h, acts, w_out, j, row3(ffn_norm_w), ffn_w_up_b, ffn_conv_w, row3(ffn_conv_b),
                 ffn_w_down_b, final_norm_w[None, :], i, seq=seq, final_norm=(i == depth - 1))
    return h.reshape(batch, seq, D_MODEL)
```

```python
import functools
import math

import jax
import jax.numpy as jnp
from jax import lax
from jax.experimental import pallas as pl
from jax.experimental.pallas import tpu as pltpu

F32 = jnp.float32
BF16 = jnp.bfloat16

D_MODEL = 1024
H_DN = 4
DK_DN = 128
DV_DN = 128
DN_CONV = 4
DN_CHUNK = 64
H_FOX = 4
D_FOX = 128
D_DIFF = 64
H_DIFF = D_MODEL // (2 * D_DIFF)
ROPE_THETA = 10000.0
D_FF = 128 * ((8 * D_MODEL // 3 + 127) // 128)
FFN_CONV = 3
EPS = 1e-6
LOG2E = 1.4426950408889634

LANES = 128
MXU_N = 256
BF16_SUBLANES = 16
VMEM_LIMIT = 56 * 1024 * 1024
VT_ROWS = LANES + BF16_SUBLANES
CB_TERMS = 3
KX_STRIDE = 4
GATE_ROWS = 16

DN_QKV = H_DN * (2 * DK_DN + DV_DN)
EV_GATE0 = DN_QKV + H_DN * DV_DN
EV_FOX0 = EV_GATE0 + 2 * H_DN
EV_BIG = EV_GATE0 + 2 * H_FOX * D_FOX
_QF, _KF = EV_GATE0 // LANES, EV_GATE0 // LANES + H_FOX
_G_BETA, _G_DECAY, _G_FOX = 0, 4, 8


def _cparams(sem):
    return pltpu.CompilerParams(dimension_semantics=sem, vmem_limit_bytes=VMEM_LIMIT)


def _resident(block_shape, index_map):
    return pl.BlockSpec(block_shape, index_map, pipeline_mode=pl.Buffered(1))


def _rms(x, w):
    return x * lax.rsqrt(jnp.mean(x * x, axis=-1, keepdims=True) + EPS) * w


def _sigmoid(x):
    return 1.0 / (1.0 + jnp.exp(-x))


def _silu(x):
    return x * _sigmoid(x)


def _dot(a, b):
    return jnp.dot(a, b, preferred_element_type=F32)


def _dot_nt(a, b):
    return lax.dot_general(a, b, (((1,), (1,)), ((), ())), preferred_element_type=F32)


def _store_vt(vt_ref, v, head0, transposed=True):
    vt = v.astype(BF16) if transposed else v.astype(BF16).T
    ones = jnp.ones((VT_ROWS - LANES, vt.shape[1]), BF16)
    for h in range(vt.shape[0] // LANES):
        row0 = (head0 + h) * VT_ROWS
        vt_ref[row0:row0 + LANES, :] = vt[h * LANES:(h + 1) * LANES]
        vt_ref[row0 + LANES:row0 + VT_ROWS, :] = ones


def _ev_proj_kernel(x_ref, nw_ref, wa_ref, wf_ref, ws_ref, big_ref, vt_ref, small_ref, hn_ref, *, tn):
    hn_ref[...] = _rms(x_ref[...], nw_ref[...]).astype(BF16)
    small_ref[...] = _dot_nt(hn_ref[...], ws_ref[...])
    for c in range(EV_GATE0 // tn):
        rows = slice(c * tn, (c + 1) * tn)
        big_ref[:, rows] = _dot_nt(hn_ref[...], wa_ref[rows, :]).astype(BF16)
    n_qk = 2 * H_FOX * D_FOX
    for c in range(n_qk // tn):
        y = _dot_nt(hn_ref[...], wf_ref[c * tn:(c + 1) * tn, :])
        if (c + 1) * tn <= H_FOX * D_FOX:
            y = y * ((D_FOX ** -0.5) * LOG2E)
        big_ref[:, EV_GATE0 + c * tn:EV_GATE0 + (c + 1) * tn] = y.astype(BF16)
    _store_vt(vt_ref, _dot_nt(wf_ref[n_qk:, :], hn_ref[...]), 0)


def _ev_proj(x, nw, w_a, w_fox, w_small, layer, *, tm=1024, tn=512):
    m = x.shape[0]
    return pl.pallas_call(
        functools.partial(_ev_proj_kernel, tn=tn),
        grid=(m // tm,),
        in_specs=[
            pl.BlockSpec((tm, D_MODEL), lambda i: (i, 0)),
            _resident((None, 1, D_MODEL), lambda i: (layer, 0, 0)),
            _resident((EV_GATE0, D_MODEL), lambda i: (0, layer)),
            _resident((w_fox.shape[0], D_MODEL), lambda i: (0, layer)),
            _resident((LANES, D_MODEL), lambda i: (0, layer)),
        ],
        out_specs=[
            pl.BlockSpec((tm, EV_BIG), lambda i: (i, 0)),
            pl.BlockSpec((H_FOX * VT_ROWS, tm), lambda i: (0, i)),
            pl.BlockSpec((tm, LANES), lambda i: (i, 0)),
        ],
        out_shape=[
            jax.ShapeDtypeStruct((m, EV_BIG), BF16),
            jax.ShapeDtypeStruct((H_FOX * VT_ROWS, m), BF16),
            jax.ShapeDtypeStruct((m, LANES), F32),
        ],
        scratch_shapes=[pltpu.VMEM((tm, D_MODEL), BF16)],
        compiler_params=_cparams(("parallel",)),
        name="ev_proj",
    )(x, nw, w_a, w_fox, w_small)


def _gate_kernel(lg_ref, nega_ref, dtb_ref, fb_ref, col_ref, row_ref, kx_ref, *, seq):
    t = lg_ref[...].T[:GATE_ROWS, :]
    ch = lax.broadcasted_iota(jnp.int32, t.shape, 0)
    pos = lax.broadcasted_iota(jnp.int32, t.shape, 1)
    is_decay = (ch >= _G_DECAY) & (ch < _G_FOX)
    is_fox = (ch >= _G_FOX) & (ch < _G_FOX + H_FOX)
    beta = _sigmoid(t)
    td = t + dtb_ref[:, :1]
    softplus = jnp.maximum(td, 0.0) + jnp.log(1.0 + jnp.exp(-jnp.abs(td)))
    g = nega_ref[:, :1] * softplus
    tf = t + fb_ref[:, :1]
    logf = (jnp.minimum(tf, 0.0) - jnp.log(1.0 + jnp.exp(-jnp.abs(tf)))) * LOG2E
    v = jnp.where(is_decay, g, jnp.where(is_fox, logf, 0.0))
    idx = jnp.where(is_decay, pos % DN_CHUNK, jnp.where(is_fox, pos, 0))
    k = 1
    while k < seq:
        shifted = pltpu.roll(v, k, 1)
        v = v + jnp.where(idx >= k, shifted, 0.0)
        k *= 2
    out = jnp.where(ch < _G_DECAY, beta, v)
    row_ref[...] = out
    pad = jnp.zeros((LANES - GATE_ROWS, seq), F32)
    col_ref[...] = jnp.concatenate([out, pad], axis=0).T
    ext = jnp.zeros(t.shape, F32)
    for h in range(H_FOX):
        rest = v[_G_FOX + h:_G_FOX + h + 1, :]
        for term in range(CB_TERMS):
            part = rest.astype(BF16).astype(F32)
            ext = jnp.where(ch == KX_STRIDE * h + term, part, ext)
            rest = rest - part
    kx_ref[...] = jnp.concatenate([ext, pad], axis=0).T.astype(BF16)


def _gates(small, nega, dtb, fb, *, batch, seq):
    return pl.pallas_call(
        functools.partial(_gate_kernel, seq=seq),
        grid=(batch,),
        in_specs=[
            pl.BlockSpec((seq, LANES), lambda b: (b, 0)),
            pl.BlockSpec((GATE_ROWS, LANES), lambda b: (0, 0)),
            pl.BlockSpec((GATE_ROWS, LANES), lambda b: (0, 0)),
            pl.BlockSpec((GATE_ROWS, LANES), lambda b: (0, 0)),
        ],
        out_specs=[
            pl.BlockSpec((seq, LANES), lambda b: (b, 0)),
            pl.BlockSpec((None, GATE_ROWS, seq), lambda b: (b, 0, 0)),
            pl.BlockSpec((None, seq, LANES), lambda b: (b, 0, 0)),
        ],
        out_shape=[
            jax.ShapeDtypeStruct((batch * seq, LANES), F32),
            jax.ShapeDtypeStruct((batch, GATE_ROWS, seq), F32),
            jax.ShapeDtypeStruct((batch, seq, LANES), BF16),
        ],
        compiler_params=_cparams(("parallel",)),
        name="gates",
    )(small, nega, dtb, fb)


def _dn_kernel(q_ref, k_ref, v_ref, qh_ref, kh_ref, vh_ref, z_ref, cq_ref, ck_ref, cv_ref,
               gcol_ref, grow_ref, nw_ref, o_ref, state_ref, *, ts):
    si = pl.program_id(1)
    c = DN_CHUNK
    nc = ts // c
    halo = BF16_SUBLANES
    heads = range(H_DN)

    @pl.when(si == 0)
    def _():
        state_ref[...] = jnp.zeros_like(state_ref)

    def conv_silu(x_ref, halo_ref, cw_ref):
        x = x_ref[...].astype(F32)
        prev = jnp.where(si == 0, 0.0, halo_ref[...].astype(F32))
        xe = jnp.concatenate([prev, x], axis=0)
        cw = cw_ref[...]
        y = cw[DN_CONV - 1:DN_CONV] * x
        for j in range(DN_CONV - 1):
            off = halo - (DN_CONV - 1) + j
            y = y + cw[j:j + 1] * xe[off:off + ts]
        return _silu(y)

    def l2n(x):
        return x * lax.rsqrt(jnp.sum(x * x, axis=-1, keepdims=True) + EPS)

    qa = conv_silu(q_ref, qh_ref, cq_ref)
    ka = conv_silu(k_ref, kh_ref, ck_ref)
    va = conv_silu(v_ref, vh_ref, cv_ref)
    gates = gcol_ref[...]
    lane = lax.broadcasted_iota(jnp.int32, gates.shape, 1)

    r2 = lax.broadcasted_iota(jnp.int32, (c, c), 0)
    c2 = lax.broadcasted_iota(jnp.int32, (c, c), 1)
    incl = r2 >= c2
    incl_f = incl.astype(F32)
    strict_f = (r2 > c2).astype(F32)

    def level_mask(s):
        hit = ((r2 // (2 * s)) == (c2 // (2 * s))) & ((r2 // s) % 2 == 1) & ((c2 // s) % 2 == 0)
        return hit.astype(F32)

    bdot_nt = lambda a, b: jnp.einsum('cid,cjd->cij', a, b, preferred_element_type=F32)
    bdot = lambda a, b: jnp.einsum('cij,cjd->cid', a, b, preferred_element_type=F32)
    chunks = lambda a: a.reshape(nc, c, a.shape[-1])

    k3, kb3, q3, rhs, qg3, kd3, egl, decay = [], [], [], [], [], [], [], []
    for h in heads:
        cols = slice(h * DK_DN, (h + 1) * DK_DN)
        q = l2n(qa[:, cols]) * (DK_DN ** -0.5)
        k = l2n(ka[:, cols])
        v = va[:, cols]
        beta = jnp.sum(jnp.where(lane == _G_BETA + h, gates, 0.0), axis=1, keepdims=True)
        gc = jnp.sum(jnp.where(lane == _G_DECAY + h, gates, 0.0), axis=1, keepdims=True)
        g_row = grow_ref[:, _G_DECAY + h:_G_DECAY + h + 1, :]
        gc3 = chunks(gc)
        gl3 = gc3[:, c - 1:c, :]
        eg = jnp.exp(gc)
        kb = k * beta
        k3.append(chunks(k).astype(BF16))
        kb3.append(chunks(kb).astype(BF16))
        q3.append(chunks(q).astype(BF16))
        rhs.append(jnp.concatenate([chunks(v * beta), chunks(kb * eg)], axis=2))
        qg3.append(chunks(q * eg))
        kd3.append((chunks(k) * jnp.exp(gl3 - gc3)).astype(BF16))
        egl.append(jnp.exp(gl3))
        decay.append(jnp.exp(jnp.where(incl[None], gc3 - g_row, 0.0)))

    lower = [bdot_nt(kb3[h], k3[h]) * (decay[h] * strict_f[None]) for h in heads]
    qk = [(bdot_nt(q3[h], k3[h]) * (decay[h] * incl_f[None])).astype(BF16) for h in heads]

    first = level_mask(1)
    xs = [-(lower[h] * first[None]) for h in heads]
    s = 2
    while s < c:
        lm = level_mask(s)
        cs = [lower[h] * lm[None] for h in heads]
        y = [cs[h] + bdot(cs[h].astype(BF16), xs[h].astype(BF16)) for h in heads]
        xs = [xs[h] - y[h] - bdot(xs[h].astype(BF16), y[h].astype(BF16)) for h in heads]
        s *= 2

    uwb = [(rhs[h] + bdot(xs[h].astype(BF16), rhs[h].astype(BF16))).astype(BF16) for h in heads]

    quw = [bdot(qk[h], uwb[h]) for h in heads]
    o_in = [quw[h][:, :, :DV_DN] for h in heads]
    q_st = [(qg3[h] - quw[h][:, :, DV_DN:]).astype(BF16) for h in heads]
    kuw = [[lax.dot_general(kd3[h][n], uwb[h][n], (((0,), (0,)), ((), ())),
                            preferred_element_type=F32) for n in range(nc)]
           for h in heads]

    state = [state_ref[h] for h in heads]
    outs = [[] for _ in heads]
    for n in range(nc):
        lhs = [jnp.concatenate([kuw[h][n][:, DV_DN:].astype(BF16), q_st[h][n]], axis=0)
               for h in heads]
        r = [_dot(lhs[h], state[h].astype(BF16)) for h in heads]
        for h in heads:
            outs[h].append(r[h][DK_DN:] + o_in[h][n])
        state = [state[h] * egl[h][n] + kuw[h][n][:, :DV_DN] - r[h][:DK_DN] for h in heads]
    for h in heads:
        cols = slice(h * DV_DN, (h + 1) * DV_DN)
        state_ref[h] = state[h]
        o = jnp.concatenate(outs[h], axis=0)
        o = _rms(o, nw_ref[...]) * _silu(z_ref[:, cols].astype(F32))
        o_ref[:, cols] = o.astype(BF16)


def _deltanet(big, conv_w, gcol, grow4, norm_w, layer, *, batch, seq, ts=512):
    m = batch * seq
    ns = seq // ts
    nc = ts // DN_CHUNK
    hb = ts // BF16_SUBLANES
    w = H_DN * DK_DN

    def blk(col):
        return pl.BlockSpec((ts, w), lambda b, s: (b * ns + s, col))

    def halo(col):
        return pl.BlockSpec((BF16_SUBLANES, w),
                            lambda b, s: (jnp.maximum((b * ns + s) * hb - 1, 0), col))

    def cw(col):
        return pl.BlockSpec((None, DN_CONV, w), lambda b, s: (layer, 0, col))

    return pl.pallas_call(
        functools.partial(_dn_kernel, ts=ts),
        grid=(batch, ns),
        in_specs=[
            blk(0), blk(1), blk(2), halo(0), halo(1), halo(2), blk(3),
            cw(0), cw(1), cw(2),
            pl.BlockSpec((ts, LANES), lambda b, s: (b * ns + s, 0)),
            pl.BlockSpec((None, nc, 16, DN_CHUNK), lambda b, s: (b, s, 0, 0)),
            pl.BlockSpec((None, 1, DV_DN), lambda b, s: (layer, 0, 0)),
        ],
        out_specs=pl.BlockSpec((ts, w), lambda b, s: (b * ns + s, 0)),
        out_shape=jax.ShapeDtypeStruct((m, H_DN * DV_DN), BF16),
        scratch_shapes=[pltpu.VMEM((H_DN, DK_DN, DV_DN), F32)],
        compiler_params=_cparams(("parallel", "arbitrary")),
        name="deltanet",
    )(big, big, big, big, big, big, big, conv_w, conv_w, conv_w, gcol, grow4, norm_w)


def _attn_kernel(*refs, tq, ck, diff, lambda_init):
    if diff:
        q_ref, k_ref, vt_ref, lv_ref, sw_ref, o_ref, qz_ref, m_ref, acc_ref = refs
        rsub = MXU_N // 2
    else:
        q_ref, k_ref, kx_ref, vt_ref, o_ref, qz_ref, m_ref, acc_ref = refs
        rsub = MXU_N
    nsub = tq // rsub
    ahead = 5
    kpi = 4 if diff else 8
    qi = pl.program_id(2)

    for r in range(nsub):
        rows = q_ref[r * rsub:(r + 1) * rsub, :]
        lane = lax.broadcasted_iota(jnp.int32, rows.shape, 1)
        if diff:
            qz_ref[r, :rsub, :] = jnp.where(lane < D_DIFF, rows, jnp.zeros_like(rows))
            qz_ref[r, rsub:, :] = jnp.where(lane >= D_DIFF, rows, jnp.zeros_like(rows))
        else:
            qz_ref[r, :, :LANES] = rows
            lane0 = KX_STRIDE * pl.program_id(1)
            own = (lane >= lane0) & (lane < lane0 + CB_TERMS)
            qz_ref[r, :, LANES:] = jnp.where(own, -1.0, 0.0).astype(BF16)
    m_ref[...] = jnp.full_like(m_ref, -jnp.inf)
    acc_ref[...] = jnp.zeros_like(acc_ref)

    def run(tiles):
        loaded = {}

        def operands(t):
            if t not in loaded:
                ks = tiles[t][0]
                k = k_ref[pl.ds(ks, ck), :]
                if not diff:
                    k = jnp.concatenate([k, kx_ref[pl.ds(ks, ck), :]], axis=1)
                loaded[t] = (k, vt_ref[:, pl.ds(ks, ck)])
            return loaded[t]

        items = [(t, r) for t, (_, plan) in enumerate(tiles) for r in range(nsub)
                 if plan[r] != "skip"]
        scores = {}

        def issue(item):
            scores[item] = _dot_nt(operands(item[0])[0], qz_ref[item[1]])

        for item in items[:ahead]:
            issue(item)
        for idx, (t, r) in enumerate(items):
            if idx + ahead < len(items):
                issue(items[idx + ahead])
            s = scores.pop((t, r))
            vt = operands(t)[1]
            plan = tiles[t][1]
            if plan[r] is not None:
                kpos = lax.broadcasted_iota(jnp.int32, s.shape, 0) + plan[r]
                qpos = lax.broadcasted_iota(jnp.int32, s.shape, 1)
                if diff:
                    qpos = qpos % rsub
                s = jnp.where(kpos <= qpos + r * rsub, s, -jnp.inf)
            m_prev = m_ref[r]
            m_new = jnp.maximum(m_prev, jnp.max(s, axis=0, keepdims=True))
            p = jnp.exp2(s - m_new)
            acc_ref[r] = jnp.exp2(m_prev - m_new) * acc_ref[r] + _dot(vt, p.astype(BF16))
            m_ref[r] = m_new

    def body(t, carry):
        base = t * (kpi * ck)
        run([(pl.multiple_of(base + u * ck, ck), [None] * nsub) for u in range(kpi)])
        return carry

    lax.fori_loop(0, qi * (tq // (kpi * ck)), body, 0)
    diag = []
    for jd in range(tq // ck):
        plan = []
        for r in range(nsub):
            if jd * ck > r * rsub + rsub - 1:
                plan.append("skip")
            elif jd * ck + ck - 1 <= r * rsub:
                plan.append(None)
            else:
                plan.append(jd * ck)
        diag.append((pl.multiple_of(qi * tq + jd * ck, ck), plan))
    run(diag)

    if diff:
        lv = lv_ref[...]
        lam = (jnp.exp(jnp.sum(lv[0:1] * lv[1:2], axis=1, keepdims=True))
               - jnp.exp(jnp.sum(lv[2:3] * lv[3:4], axis=1, keepdims=True)) + lambda_init)
    for r in range(nsub):
        tot = acc_ref[r]
        ot = tot[:LANES] / tot[LANES:LANES + 1]
        if diff:
            o = (ot[:, :rsub] - lam * ot[:, rsub:]).T
            o = _rms(o, sw_ref[...]) * (1.0 - lambda_init)
        else:
            o = ot.T
        o_ref[r * rsub:(r + 1) * rsub, :] = o.astype(BF16)


def _attn_scratch(tq, rsub, kdim):
    nsub = tq // rsub
    return [pltpu.VMEM((nsub, MXU_N, kdim), BF16), pltpu.VMEM((nsub, 1, MXU_N), F32),
            pltpu.VMEM((nsub, VT_ROWS, MXU_N), F32)]


def _fox(big, kx, vt, *, batch, seq, tq=2048, ck=256):
    nq = seq // tq
    big3 = big.reshape(batch, seq, EV_BIG)
    return pl.pallas_call(
        functools.partial(_attn_kernel, tq=tq, ck=ck, diff=False, lambda_init=0.0),
        grid=(batch, H_FOX, nq),
        in_specs=[
            pl.BlockSpec((None, tq, LANES), lambda b, h, i: (b, i, _QF + h)),
            pl.BlockSpec((None, seq, LANES), lambda b, h, i: (b, 0, _KF + h)),
            pl.BlockSpec((None, seq, LANES), lambda b, h, i: (b, 0, 0)),
            pl.BlockSpec((VT_ROWS, seq), lambda b, h, i: (h, b)),
        ],
        out_specs=pl.BlockSpec((None, tq, LANES), lambda b, h, i: (b, i, h)),
        out_shape=jax.ShapeDtypeStruct((batch, seq, H_FOX * D_FOX), BF16),
        scratch_shapes=_attn_scratch(tq, MXU_N, 2 * LANES),
        compiler_params=_cparams(("parallel", "parallel", "arbitrary")),
        name="fox_attn",
    )(big3, big3, kx, vt).reshape(batch * seq, H_FOX * D_FOX)


def _diff_attn(qk, vt, lam_vecs, subln_w, layer, lambda_init, *, batch, seq, tq=2048, ck=256):
    nq = seq // tq
    qk3 = qk.reshape(batch, seq, 2 * D_MODEL)
    return pl.pallas_call(
        functools.partial(_attn_kernel, tq=tq, ck=ck, diff=True, lambda_init=lambda_init),
        grid=(batch, H_DIFF, nq),
        in_specs=[
            pl.BlockSpec((None, tq, LANES), lambda b, h, i: (b, i, h)),
            pl.BlockSpec((None, seq, LANES), lambda b, h, i: (b, 0, H_DIFF + h)),
            pl.BlockSpec((VT_ROWS, seq), lambda b, h, i: (h, b)),
            pl.BlockSpec((None, 4, D_DIFF), lambda b, h, i: (layer, 0, 0)),
            pl.BlockSpec((None, 1, 2 * D_DIFF), lambda b, h, i: (layer, 0, 0)),
        ],
        out_specs=pl.BlockSpec((None, tq, LANES), lambda b, h, i: (b, i, h)),
        out_shape=jax.ShapeDtypeStruct((batch, seq, D_MODEL), BF16),
        scratch_shapes=_attn_scratch(tq, MXU_N // 2, LANES),
        compiler_params=_cparams(("parallel", "parallel", "arbitrary")),
        name="diff_attn",
    )(qk3, qk3, vt, lam_vecs, subln_w).reshape(batch * seq, D_MODEL)


def _od_proj_kernel(x_ref, nw_ref, w_ref, cos_ref, sin_ref, o_ref, vt_ref, hn_ref, *, tn):
    hn_ref[...] = _rms(x_ref[...], nw_ref[...]).astype(BF16)
    half = D_DIFF // 2
    lane = lax.broadcasted_iota(jnp.int32, (x_ref.shape[0], LANES), 1)
    first_half = (lane % D_DIFF) < half
    for c in range(2 * D_MODEL // tn):
        y = _dot(hn_ref[...], w_ref[:, c * tn:(c + 1) * tn])
        table = (c * tn) // D_MODEL
        cos = cos_ref[table]
        sin = sin_ref[table]
        for g in range(tn // LANES):
            yg = y[:, g * LANES:(g + 1) * LANES]
            rot = jnp.where(first_half, pltpu.roll(yg, LANES - half, 1), pltpu.roll(yg, half, 1))
            o_ref[:, c * tn + g * LANES:c * tn + (g + 1) * LANES] = (yg * cos + rot * sin).astype(BF16)
    for c in range(D_MODEL // tn):
        col0 = 2 * D_MODEL + c * tn
        _store_vt(vt_ref, _dot(hn_ref[...], w_ref[:, col0:col0 + tn]), c * (tn // LANES), transposed=False)


def _od_proj(x, nw, w, cos_t, sin_t, layer, *, seq, tm=1024, tn=512):
    m = x.shape[0]
    sb = seq // tm
    return pl.pallas_call(
        functools.partial(_od_proj_kernel, tn=tn),
        grid=(m // tm,),
        in_specs=[
            pl.BlockSpec((tm, D_MODEL), lambda i: (i, 0)),
            _resident((None, 1, D_MODEL), lambda i: (layer, 0, 0)),
            _resident((None, D_MODEL, 3 * D_MODEL), lambda i: (layer, 0, 0)),
            pl.BlockSpec((2, tm, LANES), lambda i: (0, i % sb, 0)),
            pl.BlockSpec((2, tm, LANES), lambda i: (0, i % sb, 0)),
        ],
        out_specs=[
            pl.BlockSpec((tm, 2 * D_MODEL), lambda i: (i, 0)),
            pl.BlockSpec((H_DIFF * VT_ROWS, tm), lambda i: (0, i)),
        ],
        out_shape=[
            jax.ShapeDtypeStruct((m, 2 * D_MODEL), BF16),
            jax.ShapeDtypeStruct((H_DIFF * VT_ROWS, m), BF16),
        ],
        scratch_shapes=[pltpu.VMEM((tm, D_MODEL), BF16)],
        compiler_params=_cparams(("parallel",)),
        name="od_proj",
    )(x, nw, w, cos_t, sin_t)


def _rope_tables(seq):
    half = D_DIFF // 2
    inv = 1.0 / (ROPE_THETA ** (jnp.arange(half, dtype=F32) * 2.0 / D_DIFF))
    ang = jnp.arange(seq, dtype=F32)[:, None] * inv[None, :]
    cos = jnp.tile(jnp.cos(ang), (1, LANES // half))
    sin = jnp.tile(jnp.concatenate([-jnp.sin(ang), jnp.sin(ang)], axis=1), (1, LANES // D_DIFF))
    qs = (D_DIFF ** -0.5) * LOG2E
    return jnp.stack([cos * qs, cos]), jnp.stack([sin * qs, sin])


def _ffn_kernel(*refs, n_in, tm, tn, rows_per_seq, final_norm):
    x_ref, xp_ref = refs[:2]
    a_refs = refs[2:2 + n_in]
    ap_refs = refs[2 + n_in:2 + 2 * n_in]
    wo_ref, nw_ref, wup_ref, cw_ref, cb_ref, wd_ref, fw_ref, o_ref, hn_ref, hmid_ref = refs[2 + 2 * n_in:]
    halo = BF16_SUBLANES
    kw = a_refs[0].shape[1]

    xm = x_ref[...]
    xpm = xp_ref[...]
    for r in range(n_in):
        wo = wo_ref[r * kw:(r + 1) * kw, :]
        xm = xm + _dot(a_refs[r][...], wo)
        xpm = xpm + _dot(ap_refs[r][...], wo)
    seq_start = (pl.program_id(0) % rows_per_seq) == 0
    hn_ref[:halo, :] = jnp.where(seq_start, 0.0, _rms(xpm, nw_ref[...])).astype(BF16)
    hn_ref[halo:, :] = _rms(xm, nw_ref[...]).astype(BF16)

    for c in range(D_FF // tn):
        cols = slice(c * tn, (c + 1) * tn)
        ge = _dot(hn_ref[...], wup_ref[:, cols])
        gate = cb_ref[:, cols] + cw_ref[FFN_CONV - 1:FFN_CONV, cols] * ge[halo:]
        for t in range(FFN_CONV - 1):
            off = halo - (FFN_CONV - 1) + t
            gate = gate + cw_ref[t:t + 1, cols] * ge[off:off + tm]
        val = _dot(hn_ref[halo:, :], wup_ref[:, D_FF + c * tn:D_FF + (c + 1) * tn])
        hmid_ref[:, cols] = (_silu(gate) * val).astype(BF16)

    y = xm + _dot(hmid_ref[...], wd_ref[...])
    if final_norm:
        y = _rms(y, fw_ref[...])
    o_ref[...] = y


def _ffn(x, acts, w_out, out_layer, nw, w_up, conv_w, conv_b, w_down, final_w, layer, *, seq,
         final_norm, tm=1024, tn=256):
    m = x.shape[0]
    n_in = len(acts)
    kw = acts[0].shape[1]
    hb = tm // BF16_SUBLANES
    prev = lambda i: (jnp.maximum(i * hb - 1, 0), 0)
    in_specs = [pl.BlockSpec((tm, D_MODEL), lambda i: (i, 0)),
                pl.BlockSpec((BF16_SUBLANES, D_MODEL), prev)]
    in_specs += [pl.BlockSpec((tm, kw), lambda i: (i, 0)) for _ in acts]
    in_specs += [pl.BlockSpec((BF16_SUBLANES, kw), prev) for _ in acts]
    in_specs += [
        _resident((None, D_MODEL, D_MODEL), lambda i: (out_layer, 0, 0)),
        _resident((None, 1, D_MODEL), lambda i: (layer, 0, 0)),
        _resident((None, D_MODEL, 2 * D_FF), lambda i: (layer, 0, 0)),
        _resident((None, FFN_CONV, D_FF), lambda i: (layer, 0, 0)),
        _resident((None, 1, D_FF), lambda i: (layer, 0, 0)),
        _resident((None, D_FF, D_MODEL), lambda i: (layer, 0, 0)),
        _resident((1, D_MODEL), lambda i: (0, 0)),
    ]
    return pl.pallas_call(
        functools.partial(_ffn_kernel, n_in=n_in, tm=tm, tn=tn, rows_per_seq=seq // tm,
                          final_norm=final_norm),
        grid=(m // tm,),
        in_specs=in_specs,
        out_specs=pl.BlockSpec((tm, D_MODEL), lambda i: (i, 0)),
        out_shape=jax.ShapeDtypeStruct((m, D_MODEL), F32),
        scratch_shapes=[pltpu.VMEM((tm + BF16_SUBLANES, D_MODEL), BF16),
                        pltpu.VMEM((tm, D_FF), BF16)],
        compiler_params=_cparams(("parallel",)),
        name="conv_ffn",
    )(x, x, *acts, *acts, w_out, nw, w_up, conv_w, conv_b, w_down, final_w)


def _gate_rows(vals, start):
    z = jnp.zeros((vals.shape[0], GATE_ROWS, LANES), F32)
    return z.at[:, start:start + vals.shape[1], :].set(vals[:, :, None])


def kernel(x, ev_norm_w, ev_w_in, dn_conv_w, dn_a_log, dn_dt_bias, dn_norm_w, fox_f_bias, ev_w_out,
           od_norm_w, od_w_in, diff_lambda, diff_subln_w, od_w_out,
           ffn_norm_w, ffn_w_up, ffn_conv_w, ffn_conv_b, ffn_w_down, final_norm_w):
    batch, seq, _ = x.shape
    depth = ffn_norm_w.shape[0]
    h = x.reshape(batch * seq, D_MODEL)

    fox_v1 = EV_FOX0 + 3 * H_FOX * D_FOX
    ev_t = jnp.transpose(ev_w_in, (2, 0, 1)).reshape(ev_w_in.shape[2], ev_w_in.shape[0] * D_MODEL)
    ev_w_a = ev_t[:EV_GATE0].astype(BF16)
    ev_w_fox = ev_t[EV_FOX0:fox_v1].astype(BF16)
    ev_w_small = jnp.concatenate(
        [ev_t[EV_GATE0:EV_FOX0], ev_t[fox_v1:],
         jnp.zeros((LANES - 2 * H_DN - H_FOX, ev_t.shape[1]), F32)], axis=0).astype(BF16)
    ev_w_out_b = ev_w_out.astype(BF16)
    od_w_in_b = od_w_in.astype(BF16)
    od_w_out_b = od_w_out.astype(BF16)
    ffn_w_up_b = ffn_w_up.astype(BF16)
    ffn_w_down_b = ffn_w_down.astype(BF16)
    nega = _gate_rows(-jnp.exp(dn_a_log), _G_DECAY)
    dtb = _gate_rows(dn_dt_bias, _G_DECAY)
    fb = _gate_rows(fox_f_bias, _G_FOX)
    cos_t, sin_t = _rope_tables(seq)
    row3 = lambda a: a[:, None, :]

    for i in range(depth):
        j = i // 2
        if i % 2 == 0:
            big, vt, small = _ev_proj(h, row3(ev_norm_w), ev_w_a, ev_w_fox, ev_w_small, j)
            gcol, grow, kx = _gates(small, nega[j], dtb[j], fb[j], batch=batch, seq=seq)
            grow4 = grow.reshape(batch, GATE_ROWS, seq // DN_CHUNK, DN_CHUNK).transpose(0, 2, 1, 3)
            o_d = _deltanet(big, dn_conv_w, gcol, grow4, row3(dn_norm_w), j, batch=batch, seq=seq)
            o_f = _fox(big, kx, vt, batch=batch, seq=seq)
            acts, w_out = [o_d, o_f], ev_w_out_b
        else:
            lambda_init = 0.8 - 0.6 * math.exp(-0.3 * i)
            qk, vt = _od_proj(h, row3(od_norm_w), od_w_in_b, cos_t, sin_t, j, seq=seq)
            o = _diff_attn(qk, vt, diff_lambda, row3(diff_subln_w), j, lambda_init, batch=batch, seq=seq)
            acts, w_out = [o], od_w_out_b
        h = _ffn(h, acts, w_out, j, row3(ffn_norm_w), ffn_w_up_b, ffn_conv_w, row3(ffn_conv_b),
                 ffn_w_down_b, final_norm_w[None, :], i, seq=seq, final_norm=(i == depth - 1))
    return h.reshape(batch, seq, D_MODEL)
```

```python
import functools
import math

import jax
import jax.numpy as jnp
from jax import lax
from jax.experimental import pallas as pl
from jax.experimental.pallas import tpu as pltpu

F32 = jnp.float32
BF16 = jnp.bfloat16

D_MODEL = 1024
H_DN = 4
DK_DN = 128
DV_DN = 128
DN_CONV = 4
DN_CHUNK = 64
H_FOX = 4
D_FOX = 128
D_DIFF = 64
H_DIFF = D_MODEL // (2 * D_DIFF)
ROPE_THETA = 10000.0
D_FF = 128 * ((8 * D_MODEL // 3 + 127) // 128)
FFN_CONV = 3
EPS = 1e-6
LOG2E = 1.4426950408889634

LANES = 128
MXU_N = 256
BF16_SUBLANES = 16
VMEM_LIMIT = 56 * 1024 * 1024
VT_ROWS = LANES + BF16_SUBLANES
CB_TERMS = 3
KX_STRIDE = 4
GATE_ROWS = 16
CONV_TAIL = 8

DN_QKV = H_DN * (2 * DK_DN + DV_DN)
EV_GATE0 = DN_QKV + H_DN * DV_DN
EV_FOX0 = EV_GATE0 + 2 * H_DN
EV_BIG = EV_GATE0 + 2 * H_FOX * D_FOX
_QF, _KF = EV_GATE0 // LANES, EV_GATE0 // LANES + H_FOX
_G_BETA, _G_DECAY, _G_FOX = 0, 4, 8


def _cparams(sem):
    return pltpu.CompilerParams(dimension_semantics=sem, vmem_limit_bytes=VMEM_LIMIT)


def _resident(block_shape, index_map):
    return pl.BlockSpec(block_shape, index_map, pipeline_mode=pl.Buffered(1))


def _rms(x, w):
    return x * lax.rsqrt(jnp.mean(x * x, axis=-1, keepdims=True) + EPS) * w


def _sigmoid(x):
    return 1.0 / (1.0 + jnp.exp(-x))


def _silu(x):
    return x * _sigmoid(x)


def _dot(a, b):
    return jnp.dot(a, b, preferred_element_type=F32)


def _dot_nt(a, b):
    return lax.dot_general(a, b, (((1,), (1,)), ((), ())), preferred_element_type=F32)


def _store_vt(vt_ref, v, head0, transposed=True):
    vt = v.astype(BF16) if transposed else v.astype(BF16).T
    ones = jnp.ones((VT_ROWS - LANES, vt.shape[1]), BF16)
    for h in range(vt.shape[0] // LANES):
        row0 = (head0 + h) * VT_ROWS
        vt_ref[row0:row0 + LANES, :] = vt[h * LANES:(h + 1) * LANES]
        vt_ref[row0 + LANES:row0 + VT_ROWS, :] = ones


def _ev_proj_kernel(x_ref, nw_ref, wa_ref, wf_ref, ws_ref, cw_ref, big_ref, vt_ref, small_ref,
                    hn_ref, tail_ref, *, tn, rows_per_seq):
    tm = x_ref.shape[0]
    hn_ref[...] = _rms(x_ref[...], nw_ref[...]).astype(BF16)
    small_ref[...] = _dot_nt(hn_ref[...], ws_ref[...])
    seq_start = (pl.program_id(0) % rows_per_seq) == 0

    @pl.when(pl.program_id(0) == 0)
    def _():
        tail_ref[...] = jnp.zeros_like(tail_ref)

    def conv_silu(y, c):
        prev = jnp.where(seq_start, 0.0, tail_ref[c])
        tail_ref[c] = y[tm - CONV_TAIL:, :]
        ye = jnp.concatenate([prev, y], axis=0)
        cw = cw_ref[:, c * tn:(c + 1) * tn]
        out = cw[DN_CONV - 1:DN_CONV] * y
        for j in range(DN_CONV - 1):
            off = CONV_TAIL - (DN_CONV - 1) + j
            out = out + cw[j:j + 1] * ye[off:off + tm]
        return _silu(out)

    def l2n_heads(x, scale):
        parts = []
        for h in range(tn // DK_DN):
            xh = x[:, h * DK_DN:(h + 1) * DK_DN]
            parts.append(xh * (lax.rsqrt(jnp.sum(xh * xh, axis=-1, keepdims=True) + EPS) * scale))
        return jnp.concatenate(parts, axis=1)

    def dn_chunk(c):
        rows = slice(c * tn, (c + 1) * tn)
        y = _dot_nt(hn_ref[...], wa_ref[rows, :])
        if c * tn < H_DN * DK_DN:
            y = l2n_heads(conv_silu(y, c), DK_DN ** -0.5)
        elif c * tn < 2 * H_DN * DK_DN:
            y = l2n_heads(conv_silu(y, c), 1.0)
        elif c * tn < DN_QKV:
            y = conv_silu(y, c)
        big_ref[:, rows] = y.astype(BF16)

    n_qk = 2 * H_FOX * D_FOX

    def fox_chunk(c):
        y = _dot_nt(hn_ref[...], wf_ref[c * tn:(c + 1) * tn, :])
        if (c + 1) * tn <= H_FOX * D_FOX:
            y = y * ((D_FOX ** -0.5) * LOG2E)
        big_ref[:, EV_GATE0 + c * tn:EV_GATE0 + (c + 1) * tn] = y.astype(BF16)

    plain = [functools.partial(fox_chunk, c) for c in range(n_qk // tn)]
    plain.append(lambda: _store_vt(vt_ref, _dot_nt(wf_ref[n_qk:, :], hn_ref[...]), 0))
    for c in range(EV_GATE0 // tn):
        dn_chunk(c)
        if c * tn < DN_QKV and plain:
            plain.pop(0)()
    for chunk in plain:
        chunk()


def _ev_proj(x, nw, w_a, w_fox, w_small, conv_w, layer, *, seq, tm=512, tn=512):
    m = x.shape[0]
    return pl.pallas_call(
        functools.partial(_ev_proj_kernel, tn=tn, rows_per_seq=seq // tm),
        grid=(m // tm,),
        in_specs=[
            pl.BlockSpec((tm, D_MODEL), lambda i: (i, 0)),
            _resident((None, 1, D_MODEL), lambda i: (layer, 0, 0)),
            _resident((EV_GATE0, D_MODEL), lambda i: (0, layer)),
            _resident((w_fox.shape[0], D_MODEL), lambda i: (0, layer)),
            _resident((LANES, D_MODEL), lambda i: (0, layer)),
            _resident((None, DN_CONV, DN_QKV), lambda i: (layer, 0, 0)),
        ],
        out_specs=[
            pl.BlockSpec((tm, EV_BIG), lambda i: (i, 0)),
            pl.BlockSpec((H_FOX * VT_ROWS, tm), lambda i: (0, i)),
            pl.BlockSpec((tm, LANES), lambda i: (i, 0)),
        ],
        out_shape=[
            jax.ShapeDtypeStruct((m, EV_BIG), BF16),
            jax.ShapeDtypeStruct((H_FOX * VT_ROWS, m), BF16),
            jax.ShapeDtypeStruct((m, LANES), F32),
        ],
        scratch_shapes=[pltpu.VMEM((tm, D_MODEL), BF16),
                        pltpu.VMEM((DN_QKV // tn, CONV_TAIL, tn), F32)],
        compiler_params=_cparams(("arbitrary",)),
        name="ev_proj",
    )(x, nw, w_a, w_fox, w_small, conv_w)


def _gate_kernel(lg_ref, nega_ref, dtb_ref, fb_ref, col_ref, row_ref, kx_ref, *, seq):
    t = lg_ref[...].T[:GATE_ROWS, :]
    ch = lax.broadcasted_iota(jnp.int32, t.shape, 0)
    pos = lax.broadcasted_iota(jnp.int32, t.shape, 1)
    is_decay = (ch >= _G_DECAY) & (ch < _G_FOX)
    is_fox = (ch >= _G_FOX) & (ch < _G_FOX + H_FOX)
    beta = _sigmoid(t)
    td = t + dtb_ref[:, :1]
    softplus = jnp.maximum(td, 0.0) + jnp.log(1.0 + jnp.exp(-jnp.abs(td)))
    g = nega_ref[:, :1] * softplus
    tf = t + fb_ref[:, :1]
    logf = (jnp.minimum(tf, 0.0) - jnp.log(1.0 + jnp.exp(-jnp.abs(tf)))) * LOG2E
    v = jnp.where(is_decay, g, jnp.where(is_fox, logf, 0.0))
    idx = jnp.where(is_decay, pos % DN_CHUNK, jnp.where(is_fox, pos, 0))
    k = 1
    while k < seq:
        shifted = pltpu.roll(v, k, 1)
        v = v + jnp.where(idx >= k, shifted, 0.0)
        k *= 2
    out = jnp.where(ch < _G_DECAY, beta, v)
    row_ref[...] = out
    pad = jnp.zeros((LANES - GATE_ROWS, seq), F32)
    col_ref[...] = jnp.concatenate([out, pad], axis=0).T
    ext = jnp.zeros(t.shape, F32)
    for h in range(H_FOX):
        rest = v[_G_FOX + h:_G_FOX + h + 1, :]
        for term in range(CB_TERMS):
            part = rest.astype(BF16).astype(F32)
            ext = jnp.where(ch == KX_STRIDE * h + term, part, ext)
            rest = rest - part
    kx_ref[...] = jnp.concatenate([ext, pad], axis=0).T.astype(BF16)


def _gates(small, nega, dtb, fb, *, batch, seq):
    return pl.pallas_call(
        functools.partial(_gate_kernel, seq=seq),
        grid=(batch,),
        in_specs=[
            pl.BlockSpec((seq, LANES), lambda b: (b, 0)),
            pl.BlockSpec((GATE_ROWS, LANES), lambda b: (0, 0)),
            pl.BlockSpec((GATE_ROWS, LANES), lambda b: (0, 0)),
            pl.BlockSpec((GATE_ROWS, LANES), lambda b: (0, 0)),
        ],
        out_specs=[
            pl.BlockSpec((seq, LANES), lambda b: (b, 0)),
            pl.BlockSpec((None, GATE_ROWS, seq), lambda b: (b, 0, 0)),
            pl.BlockSpec((None, seq, LANES), lambda b: (b, 0, 0)),
        ],
        out_shape=[
            jax.ShapeDtypeStruct((batch * seq, LANES), F32),
            jax.ShapeDtypeStruct((batch, GATE_ROWS, seq), F32),
            jax.ShapeDtypeStruct((batch, seq, LANES), BF16),
        ],
        compiler_params=_cparams(("parallel",)),
        name="gates",
    )(small, nega, dtb, fb)


def _dn_kernel(q_ref, k_ref, v_ref, z_ref, gcol_ref, grow_ref, nw_ref, o_ref, state_ref, *, ts):
    si = pl.program_id(1)
    c = DN_CHUNK
    nc = ts // c
    heads = range(H_DN)

    @pl.when(si == 0)
    def _():
        state_ref[...] = jnp.zeros_like(state_ref)

    gates = gcol_ref[...]
    lane = lax.broadcasted_iota(jnp.int32, gates.shape, 1)

    r2 = lax.broadcasted_iota(jnp.int32, (c, c), 0)
    c2 = lax.broadcasted_iota(jnp.int32, (c, c), 1)
    incl = r2 >= c2
    incl_f = incl.astype(F32)
    strict_f = (r2 > c2).astype(F32)

    def level_mask(s):
        hit = ((r2 // (2 * s)) == (c2 // (2 * s))) & ((r2 // s) % 2 == 1) & ((c2 // s) % 2 == 0)
        return hit.astype(F32)

    bdot_nt = lambda a, b: jnp.einsum('cid,cjd->cij', a, b, preferred_element_type=F32)
    bdot = lambda a, b: jnp.einsum('cij,cjd->cid', a, b, preferred_element_type=F32)
    chunks = lambda a: a.reshape(nc, c, a.shape[-1])

    k3, kb3, q3, rhs, qg3, kd3, egl, decay = [], [], [], [], [], [], [], []
    for h in heads:
        cols = slice(h * DK_DN, (h + 1) * DK_DN)
        q = q_ref[:, cols].astype(F32)
        k = k_ref[:, cols].astype(F32)
        v = v_ref[:, cols].astype(F32)
        beta = jnp.sum(jnp.where(lane == _G_BETA + h, gates, 0.0), axis=1, keepdims=True)
        gc = jnp.sum(jnp.where(lane == _G_DECAY + h, gates, 0.0), axis=1, keepdims=True)
        g_row = grow_ref[:, _G_DECAY + h:_G_DECAY + h + 1, :]
        gc3 = chunks(gc)
        gl3 = gc3[:, c - 1:c, :]
        eg = jnp.exp(gc)
        kb = k * beta
        k3.append(chunks(k_ref[:, cols]))
        kb3.append(chunks(kb).astype(BF16))
        q3.append(chunks(q_ref[:, cols]))
        rhs.append(jnp.concatenate([chunks(v * beta), chunks(kb * eg)], axis=2))
        qg3.append(chunks(q * eg))
        kd3.append((chunks(k) * jnp.exp(gl3 - gc3)).astype(BF16))
        egl.append(jnp.exp(gl3))
        decay.append(jnp.exp(jnp.where(incl[None], gc3 - g_row, 0.0)))

    lower = [bdot_nt(kb3[h], k3[h]) * (decay[h] * strict_f[None]) for h in heads]
    qk = [(bdot_nt(q3[h], k3[h]) * (decay[h] * incl_f[None])).astype(BF16) for h in heads]

    first = level_mask(1)
    xs = [-(lower[h] * first[None]) for h in heads]
    s = 2
    while s < c:
        lm = level_mask(s)
        cs = [lower[h] * lm[None] for h in heads]
        y = [cs[h] + bdot(cs[h].astype(BF16), xs[h].astype(BF16)) for h in heads]
        xs = [xs[h] - y[h] - bdot(xs[h].astype(BF16), y[h].astype(BF16)) for h in heads]
        s *= 2

    uwb = [(rhs[h] + bdot(xs[h].astype(BF16), rhs[h].astype(BF16))).astype(BF16) for h in heads]

    quw = [bdot(qk[h], uwb[h]) for h in heads]
    o_in = [quw[h][:, :, :DV_DN] for h in heads]
    q_st = [(qg3[h] - quw[h][:, :, DV_DN:]).astype(BF16) for h in heads]
    kuw = [[lax.dot_general(kd3[h][n], uwb[h][n], (((0,), (0,)), ((), ())),
                            preferred_element_type=F32) for n in range(nc)]
           for h in heads]

    state = [state_ref[h] for h in heads]
    outs = [[] for _ in heads]
    for n in range(nc):
        lhs = [jnp.concatenate([kuw[h][n][:, DV_DN:].astype(BF16), q_st[h][n]], axis=0)
               for h in heads]
        r = [_dot(lhs[h], state[h].astype(BF16)) for h in heads]
        for h in heads:
            outs[h].append(r[h][DK_DN:] + o_in[h][n])
        state = [state[h] * egl[h][n] + kuw[h][n][:, :DV_DN] - r[h][:DK_DN] for h in heads]
    for h in heads:
        cols = slice(h * DV_DN, (h + 1) * DV_DN)
        state_ref[h] = state[h]
        o = jnp.concatenate(outs[h], axis=0)
        o = _rms(o, nw_ref[...]) * _silu(z_ref[:, cols].astype(F32))
        o_ref[:, cols] = o.astype(BF16)


def _deltanet(big, gcol, grow4, norm_w, layer, *, batch, seq, ts=512):
    m = batch * seq
    ns = seq // ts
    nc = ts // DN_CHUNK
    w = H_DN * DK_DN

    def blk(col):
        return pl.BlockSpec((ts, w), lambda b, s: (b * ns + s, col))

    return pl.pallas_call(
        functools.partial(_dn_kernel, ts=ts),
        grid=(batch, ns),
        in_specs=[
            blk(0), blk(1), blk(2), blk(3),
            pl.BlockSpec((ts, LANES), lambda b, s: (b * ns + s, 0)),
            pl.BlockSpec((None, nc, GATE_ROWS, DN_CHUNK), lambda b, s: (b, s, 0, 0)),
            pl.BlockSpec((None, 1, DV_DN), lambda b, s: (layer, 0, 0)),
        ],
        out_specs=pl.BlockSpec((ts, w), lambda b, s: (b * ns + s, 0)),
        out_shape=jax.ShapeDtypeStruct((m, H_DN * DV_DN), BF16),
        scratch_shapes=[pltpu.VMEM((H_DN, DK_DN, DV_DN), F32)],
        compiler_params=_cparams(("parallel", "arbitrary")),
        name="deltanet",
    )(big, big, big, big, gcol, grow4, norm_w)


def _attn_kernel(*refs, tq, ck, diff, lambda_init):
    if diff:
        q_ref, k_ref, vt_ref, lv_ref, sw_ref, o_ref, qz_ref, m_ref, acc_ref = refs
        rsub = MXU_N // 2
    else:
        q_ref, k_ref, kx_ref, vt_ref, o_ref, qz_ref, m_ref, acc_ref = refs
        rsub = MXU_N
    nsub = tq // rsub
    ahead = 5
    kpi = 4 if diff else 8
    qi = pl.program_id(2)

    for r in range(nsub):
        rows = q_ref[r * rsub:(r + 1) * rsub, :]
        lane = lax.broadcasted_iota(jnp.int32, rows.shape, 1)
        if diff:
            qz_ref[r, :rsub, :] = jnp.where(lane < D_DIFF, rows, jnp.zeros_like(rows))
            qz_ref[r, rsub:, :] = jnp.where(lane >= D_DIFF, rows, jnp.zeros_like(rows))
        else:
            qz_ref[r, :, :LANES] = rows
            lane0 = KX_STRIDE * pl.program_id(1)
            own = (lane >= lane0) & (lane < lane0 + CB_TERMS)
            qz_ref[r, :, LANES:] = jnp.where(own, -1.0, 0.0).astype(BF16)
    m_ref[...] = jnp.full_like(m_ref, -jnp.inf)
    acc_ref[...] = jnp.zeros_like(acc_ref)

    def run(tiles):
        loaded = {}

        def operands(t):
            if t not in loaded:
                ks = tiles[t][0]
                k = k_ref[pl.ds(ks, ck), :]
                if not diff:
                    k = jnp.concatenate([k, kx_ref[pl.ds(ks, ck), :]], axis=1)
                loaded[t] = (k, vt_ref[:, pl.ds(ks, ck)])
            return loaded[t]

        items = [(t, r) for t, (_, plan) in enumerate(tiles) for r in range(nsub)
                 if plan[r] != "skip"]
        scores = {}

        def issue(item):
            scores[item] = _dot_nt(operands(item[0])[0], qz_ref[item[1]])

        for item in items[:ahead]:
            issue(item)
        for idx, (t, r) in enumerate(items):
            if idx + ahead < len(items):
                issue(items[idx + ahead])
            s = scores.pop((t, r))
            vt = operands(t)[1]
            plan = tiles[t][1]
            if plan[r] is not None:
                kpos = lax.broadcasted_iota(jnp.int32, s.shape, 0) + plan[r]
                qpos = lax.broadcasted_iota(jnp.int32, s.shape, 1)
                if diff:
                    qpos = qpos % rsub
                s = jnp.where(kpos <= qpos + r * rsub, s, -jnp.inf)
            m_prev = m_ref[r]
            m_new = jnp.maximum(m_prev, jnp.max(s, axis=0, keepdims=True))
            p = jnp.exp2(s - m_new)
            acc_ref[r] = jnp.exp2(m_prev - m_new) * acc_ref[r] + _dot(vt, p.astype(BF16))
            m_ref[r] = m_new

    def body(t, carry):
        base = t * (kpi * ck)
        run([(pl.multiple_of(base + u * ck, ck), [None] * nsub) for u in range(kpi)])
        return carry

    lax.fori_loop(0, qi * (tq // (kpi * ck)), body, 0)
    diag = []
    for jd in range(tq // ck):
        plan = []
        for r in range(nsub):
            if jd * ck > r * rsub + rsub - 1:
                plan.append("skip")
            elif jd * ck + ck - 1 <= r * rsub:
                plan.append(None)
            else:
                plan.append(jd * ck)
        diag.append((pl.multiple_of(qi * tq + jd * ck, ck), plan))
    run(diag)

    if diff:
        lv = lv_ref[...]
        lam = (jnp.exp(jnp.sum(lv[0:1] * lv[1:2], axis=1, keepdims=True))
               - jnp.exp(jnp.sum(lv[2:3] * lv[3:4], axis=1, keepdims=True)) + lambda_init)
    for r in range(nsub):
        tot = acc_ref[r]
        ot = tot[:LANES] / tot[LANES:LANES + 1]
        if diff:
            o = (ot[:, :rsub] - lam * ot[:, rsub:]).T
            o = _rms(o, sw_ref[...]) * (1.0 - lambda_init)
        else:
            o = ot.T
        o_ref[r * rsub:(r + 1) * rsub, :] = o.astype(BF16)


def _attn_scratch(tq, rsub, kdim):
    nsub = tq // rsub
    return [pltpu.VMEM((nsub, MXU_N, kdim), BF16), pltpu.VMEM((nsub, 1, MXU_N), F32),
            pltpu.VMEM((nsub, VT_ROWS, MXU_N), F32)]


def _fox(big, kx, vt, *, batch, seq, tq=2048, ck=256):
    nq = seq // tq
    big3 = big.reshape(batch, seq, EV_BIG)
    return pl.pallas_call(
        functools.partial(_attn_kernel, tq=tq, ck=ck, diff=False, lambda_init=0.0),
        grid=(batch, H_FOX, nq),
        in_specs=[
            pl.BlockSpec((None, tq, LANES), lambda b, h, i: (b, i, _QF + h)),
            pl.BlockSpec((None, seq, LANES), lambda b, h, i: (b, 0, _KF + h)),
            pl.BlockSpec((None, seq, LANES), lambda b, h, i: (b, 0, 0)),
            pl.BlockSpec((VT_ROWS, seq), lambda b, h, i: (h, b)),
        ],
        out_specs=pl.BlockSpec((None, tq, LANES), lambda b, h, i: (b, i, h)),
        out_shape=jax.ShapeDtypeStruct((batch, seq, H_FOX * D_FOX), BF16),
        scratch_shapes=_attn_scratch(tq, MXU_N, 2 * LANES),
        compiler_params=_cparams(("parallel", "parallel", "arbitrary")),
        name="fox_attn",
    )(big3, big3, kx, vt).reshape(batch * seq, H_FOX * D_FOX)


def _diff_attn(qk, vt, lam_vecs, subln_w, layer, lambda_init, *, batch, seq, tq=2048, ck=256):
    nq = seq // tq
    qk3 = qk.reshape(batch, seq, 2 * D_MODEL)
    return pl.pallas_call(
        functools.partial(_attn_kernel, tq=tq, ck=ck, diff=True, lambda_init=lambda_init),
        grid=(batch, H_DIFF, nq),
        in_specs=[
            pl.BlockSpec((None, tq, LANES), lambda b, h, i: (b, i, h)),
            pl.BlockSpec((None, seq, LANES), lambda b, h, i: (b, 0, H_DIFF + h)),
            pl.BlockSpec((VT_ROWS, seq), lambda b, h, i: (h, b)),
            pl.BlockSpec((None, 4, D_DIFF), lambda b, h, i: (layer, 0, 0)),
            pl.BlockSpec((None, 1, 2 * D_DIFF), lambda b, h, i: (layer, 0, 0)),
        ],
        out_specs=pl.BlockSpec((None, tq, LANES), lambda b, h, i: (b, i, h)),
        out_shape=jax.ShapeDtypeStruct((batch, seq, D_MODEL), BF16),
        scratch_shapes=_attn_scratch(tq, MXU_N // 2, LANES),
        compiler_params=_cparams(("parallel", "parallel", "arbitrary")),
        name="diff_attn",
    )(qk3, qk3, vt, lam_vecs, subln_w).reshape(batch * seq, D_MODEL)


def _od_proj_kernel(x_ref, nw_ref, w_ref, cos_ref, sin_ref, o_ref, vt_ref, hn_ref, *, tn):
    hn_ref[...] = _rms(x_ref[...], nw_ref[...]).astype(BF16)
    half = D_DIFF // 2
    lane = lax.broadcasted_iota(jnp.int32, (x_ref.shape[0], LANES), 1)
    first_half = (lane % D_DIFF) < half
    for c in range(2 * D_MODEL // tn):
        y = _dot(hn_ref[...], w_ref[:, c * tn:(c + 1) * tn])
        table = (c * tn) // D_MODEL
        cos = cos_ref[table]
        sin = sin_ref[table]
        for g in range(tn // LANES):
            yg = y[:, g * LANES:(g + 1) * LANES]
            rot = jnp.where(first_half, pltpu.roll(yg, LANES - half, 1), pltpu.roll(yg, half, 1))
            o_ref[:, c * tn + g * LANES:c * tn + (g + 1) * LANES] = (yg * cos + rot * sin).astype(BF16)
    for c in range(D_MODEL // tn):
        col0 = 2 * D_MODEL + c * tn
        _store_vt(vt_ref, _dot(hn_ref[...], w_ref[:, col0:col0 + tn]), c * (tn // LANES), transposed=False)


def _od_proj(x, nw, w, cos_t, sin_t, layer, *, seq, tm=1024, tn=512):
    m = x.shape[0]
    sb = seq // tm
    return pl.pallas_call(
        functools.partial(_od_proj_kernel, tn=tn),
        grid=(m // tm,),
        in_specs=[
            pl.BlockSpec((tm, D_MODEL), lambda i: (i, 0)),
            _resident((None, 1, D_MODEL), lambda i: (layer, 0, 0)),
            _resident((None, D_MODEL, 3 * D_MODEL), lambda i: (layer, 0, 0)),
            pl.BlockSpec((2, tm, LANES), lambda i: (0, i % sb, 0)),
            pl.BlockSpec((2, tm, LANES), lambda i: (0, i % sb, 0)),
        ],
        out_specs=[
            pl.BlockSpec((tm, 2 * D_MODEL), lambda i: (i, 0)),
            pl.BlockSpec((H_DIFF * VT_ROWS, tm), lambda i: (0, i)),
        ],
        out_shape=[
            jax.ShapeDtypeStruct((m, 2 * D_MODEL), BF16),
            jax.ShapeDtypeStruct((H_DIFF * VT_ROWS, m), BF16),
        ],
        scratch_shapes=[pltpu.VMEM((tm, D_MODEL), BF16)],
        compiler_params=_cparams(("parallel",)),
        name="od_proj",
    )(x, nw, w, cos_t, sin_t)


def _rope_tables(seq):
    half = D_DIFF // 2
    inv = 1.0 / (ROPE_THETA ** (jnp.arange(half, dtype=F32) * 2.0 / D_DIFF))
    ang = jnp.arange(seq, dtype=F32)[:, None] * inv[None, :]
    cos = jnp.tile(jnp.cos(ang), (1, LANES // half))
    sin = jnp.tile(jnp.concatenate([-jnp.sin(ang), jnp.sin(ang)], axis=1), (1, LANES // D_DIFF))
    qs = (D_DIFF ** -0.5) * LOG2E
    return jnp.stack([cos * qs, cos]), jnp.stack([sin * qs, sin])


def _ffn_kernel(*refs, n_in, tm, tn, rows_per_seq, final_norm):
    x_ref, xp_ref = refs[:2]
    a_refs = refs[2:2 + n_in]
    ap_refs = refs[2 + n_in:2 + 2 * n_in]
    wo_ref, nw_ref, wup_ref, cw_ref, cb_ref, wd_ref, fw_ref, o_ref, hn_ref, hmid_ref = refs[2 + 2 * n_in:]
    halo = BF16_SUBLANES
    kw = a_refs[0].shape[1]

    xm = x_ref[...]
    xpm = xp_ref[...]
    for r in range(n_in):
        wo = wo_ref[r * kw:(r + 1) * kw, :]
        xm = xm + _dot(a_refs[r][...], wo)
        xpm = xpm + _dot(ap_refs[r][...], wo)
    seq_start = (pl.program_id(0) % rows_per_seq) == 0
    hn_ref[:halo, :] = jnp.where(seq_start, 0.0, _rms(xpm, nw_ref[...])).astype(BF16)
    hn_ref[halo:, :] = _rms(xm, nw_ref[...]).astype(BF16)

    for c in range(D_FF // tn):
        cols = slice(c * tn, (c + 1) * tn)
        ge = _dot(hn_ref[...], wup_ref[:, cols])
        gate = cb_ref[:, cols] + cw_ref[FFN_CONV - 1:FFN_CONV, cols] * ge[halo:]
        for t in range(FFN_CONV - 1):
            off = halo - (FFN_CONV - 1) + t
            gate = gate + cw_ref[t:t + 1, cols] * ge[off:off + tm]
        val = _dot(hn_ref[halo:, :], wup_ref[:, D_FF + c * tn:D_FF + (c + 1) * tn])
        hmid_ref[:, cols] = (_silu(gate) * val).astype(BF16)

    y = xm + _dot(hmid_ref[...], wd_ref[...])
    if final_norm:
        y = _rms(y, fw_ref[...])
    o_ref[...] = y


def _ffn(x, acts, w_out, out_layer, nw, w_up, conv_w, conv_b, w_down, final_w, layer, *, seq,
         final_norm, tm=1024, tn=256):
    m = x.shape[0]
    n_in = len(acts)
    kw = acts[0].shape[1]
    hb = tm // BF16_SUBLANES
    prev = lambda i: (jnp.maximum(i * hb - 1, 0), 0)
    in_specs = [pl.BlockSpec((tm, D_MODEL), lambda i: (i, 0)),
                pl.BlockSpec((BF16_SUBLANES, D_MODEL), prev)]
    in_specs += [pl.BlockSpec((tm, kw), lambda i: (i, 0)) for _ in acts]
    in_specs += [pl.BlockSpec((BF16_SUBLANES, kw), prev) for _ in acts]
    in_specs += [
        _resident((None, D_MODEL, D_MODEL), lambda i: (out_layer, 0, 0)),
        _resident((None, 1, D_MODEL), lambda i: (layer, 0, 0)),
        _resident((None, D_MODEL, 2 * D_FF), lambda i: (layer, 0, 0)),
        _resident((None, FFN_CONV, D_FF), lambda i: (layer, 0, 0)),
        _resident((None, 1, D_FF), lambda i: (layer, 0, 0)),
        _resident((None, D_FF, D_MODEL), lambda i: (layer, 0, 0)),
        _resident((1, D_MODEL), lambda i: (0, 0)),
    ]
    return pl.pallas_call(
        functools.partial(_ffn_kernel, n_in=n_in, tm=tm, tn=tn, rows_per_seq=seq // tm,
                          final_norm=final_norm),
        grid=(m // tm,),
        in_specs=in_specs,
        out_specs=pl.BlockSpec((tm, D_MODEL), lambda i: (i, 0)),
        out_shape=jax.ShapeDtypeStruct((m, D_MODEL), F32),
        scratch_shapes=[pltpu.VMEM((tm + BF16_SUBLANES, D_MODEL), BF16),
                        pltpu.VMEM((tm, D_FF), BF16)],
        compiler_params=_cparams(("parallel",)),
        name="conv_ffn",
    )(x, x, *acts, *acts, w_out, nw, w_up, conv_w, conv_b, w_down, final_w)


def _gate_rows(vals, start):
    z = jnp.zeros((vals.shape[0], GATE_ROWS, LANES), F32)
    return z.at[:, start:start + vals.shape[1], :].set(vals[:, :, None])


def kernel(x, ev_norm_w, ev_w_in, dn_conv_w, dn_a_log, dn_dt_bias, dn_norm_w, fox_f_bias, ev_w_out,
           od_norm_w, od_w_in, diff_lambda, diff_subln_w, od_w_out,
           ffn_norm_w, ffn_w_up, ffn_conv_w, ffn_conv_b, ffn_w_down, final_norm_w):
    batch, seq, _ = x.shape
    depth = ffn_norm_w.shape[0]
    h = x.reshape(batch * seq, D_MODEL)

    fox_v1 = EV_FOX0 + 3 * H_FOX * D_FOX
    ev_t = jnp.transpose(ev_w_in, (2, 0, 1)).reshape(ev_w_in.shape[2], ev_w_in.shape[0] * D_MODEL)
    ev_w_a = ev_t[:EV_GATE0].astype(BF16)
    ev_w_fox = ev_t[EV_FOX0:fox_v1].astype(BF16)
    ev_w_small = jnp.concatenate(
        [ev_t[EV_GATE0:EV_FOX0], ev_t[fox_v1:],
         jnp.zeros((LANES - 2 * H_DN - H_FOX, ev_t.shape[1]), F32)], axis=0).astype(BF16)
    ev_w_out_b = ev_w_out.astype(BF16)
    od_w_in_b = od_w_in.astype(BF16)
    od_w_out_b = od_w_out.astype(BF16)
    ffn_w_up_b = ffn_w_up.astype(BF16)
    ffn_w_down_b = ffn_w_down.astype(BF16)
    nega = _gate_rows(-jnp.exp(dn_a_log), _G_DECAY)
    dtb = _gate_rows(dn_dt_bias, _G_DECAY)
    fb = _gate_rows(fox_f_bias, _G_FOX)
    cos_t, sin_t = _rope_tables(seq)
    row3 = lambda a: a[:, None, :]

    for i in range(depth):
        j = i // 2
        if i % 2 == 0:
            big, vt, small = _ev_proj(h, row3(ev_norm_w), ev_w_a, ev_w_fox, ev_w_small, dn_conv_w, j,
                                      seq=seq)
            gcol, grow, kx = _gates(small, nega[j], dtb[j], fb[j], batch=batch, seq=seq)
            grow4 = grow.reshape(batch, GATE_ROWS, seq // DN_CHUNK, DN_CHUNK).transpose(0, 2, 1, 3)
            o_d = _deltanet(big, gcol, grow4, row3(dn_norm_w), j, batch=batch, seq=seq)
            o_f = _fox(big, kx, vt, batch=batch, seq=seq)
            acts, w_out = [o_d, o_f], ev_w_out_b
        else:
            lambda_init = 0.8 - 0.6 * math.exp(-0.3 * i)
            qk, vt = _od_proj(h, row3(od_norm_w), od_w_in_b, cos_t, sin_t, j, seq=seq)
            o = _diff_attn(qk, vt, diff_lambda, row3(diff_subln_w), j, lambda_init, batch=batch, seq=seq)
            acts, w_out = [o], od_w_out_b
        h = _ffn(h, acts, w_out, j, row3(ffn_norm_w), ffn_w_up_b, ffn_conv_w, row3(ffn_conv_b),
                 ffn_w_down_b, final_norm_w[None, :], i, seq=seq, final_norm=(i == depth - 1))
    return h.reshape(batch, seq, D_MODEL)
```
